```python
import math
import jax, jax.numpy as jnp
from jax import lax
import numpy as np

D_MODEL = 1024
BATCH = 2
SEQ = 8192
DEPTH = 4

GRID_W = 64
CTX_LEN = 256
EPS = 1e-6

SSM_W = 512
SSM_GROUP = 16
SSM_GROUPS = SSM_W // SSM_GROUP
SSM_STATE = 64
MLA_HEADS = 8
MLA_Q_RANK = 384
MLA_KV_RANK = 256
MLA_NOPE = 64
MLA_ROPE = 32
MLA_V = 64
MLA_QK = MLA_NOPE + MLA_ROPE
ROPE_THETA = 10000.0
Q_BLOCK = 128
HY_W = 512
HY_ORDER = 2
HY_BANDS = 16
HY_EMB = 1 + 2 * HY_BANDS
HY_HIDDEN = 64
HY_CONV = 3
HY_DECAY_MIN = -math.log(1e-2) / 1.5
HY_DECAY_MAX = -math.log(1e-2) / 0.3
N_BRANCH = 3
IN_SPLITS = (SSM_W, MLA_Q_RANK, MLA_KV_RANK, MLA_ROPE, 3 * HY_W)
IN_COLS = sum(IN_SPLITS) + N_BRANCH * D_MODEL
N_EXPERTS = 16
N_EXPERT_GROUPS = 4
EXPERTS_PER_GROUP = N_EXPERTS // N_EXPERT_GROUPS
TOP_K = 2
D_EXPERT = 512

kernel_name = "hybrid_s5_mla_hyena_moe_dit"


def rms_norm(x, g):
    xf = x.astype(jnp.float32)
    y = xf * lax.rsqrt(jnp.mean(xf * xf, -1, keepdims=True) + EPS)
    return (y * g.astype(jnp.float32)).astype(x.dtype)


def modulate(x, g, shift, scale):
    return rms_norm(x, g) * (1.0 + scale) + shift


def axial_rope_tables(n_tokens):
    rows = n_tokens // GRID_W
    row = jnp.broadcast_to(jnp.arange(rows, dtype=jnp.float32)[:, None], (rows, GRID_W)).reshape(n_tokens)
    col = jnp.broadcast_to(jnp.arange(GRID_W, dtype=jnp.float32)[None, :], (rows, GRID_W)).reshape(n_tokens)
    n_f = MLA_ROPE // 4
    inv = ROPE_THETA ** (-jnp.arange(n_f, dtype=jnp.float32) / n_f)
    ang = jnp.concatenate([row[:, None] * inv, col[:, None] * inv], -1)
    return jnp.cos(ang), jnp.sin(ang)


def apply_rope_part(t, rope):
    cos, sin = rope
    cos, sin = cos[None, :, None, :], sin[None, :, None, :]
    h = MLA_ROPE // 2
    nope, r1, r2 = t[..., :MLA_NOPE], t[..., MLA_NOPE:MLA_NOPE + h], t[..., MLA_NOPE + h:]
    out = jnp.concatenate([nope, r1 * cos - r2 * sin, r2 * cos + r1 * sin], -1)
    return out.astype(t.dtype)


def s5_discretize(lam_re, lam_im, log_dt, b_re, b_im):
    lr = jnp.minimum(lam_re.astype(jnp.float32), -1e-4)
    li = lam_im.astype(jnp.float32)
    dt = jnp.exp(log_dt.astype(jnp.float32))[:, None]
    mag = jnp.exp(lr * dt)
    ar, ai = mag * jnp.cos(li * dt), mag * jnp.sin(li * dt)
    den = lr * lr + li * li
    gr = ((ar - 1.0) * lr + ai * li) / den
    gi = (ai * lr - (ar - 1.0) * li) / den
    br, bi = b_re.astype(jnp.float32), b_im.astype(jnp.float32)
    bbr = gr[..., None] * br - gi[..., None] * bi
    bbi = gr[..., None] * bi + gi[..., None] * br
    return ar, ai, bbr, bbi


def complex_scan(ar, ai, bur, bui, h0r, h0i):
    bur = bur.at[:, 0].add(ar * h0r - ai * h0i)
    bui = bui.at[:, 0].add(ar * h0i + ai * h0r)
    a_r = jnp.broadcast_to(ar, bur.shape)
    a_i = jnp.broadcast_to(ai, bur.shape)

    def combine(e1, e2):
        a1r, a1i, b1r, b1i = e1
        a2r, a2i, b2r, b2i = e2
        return (a2r * a1r - a2i * a1i, a2r * a1i + a2i * a1r,
                a2r * b1r - a2i * b1i + b2r, a2r * b1i + a2i * b1r + b2i)

    _, _, hr, hi = lax.associative_scan(combine, (a_r, a_i, bur, bui), axis=1)
    return hr, hi


def s5_direction(uc, ul, reverse, lam_re, lam_im, log_dt, b_re, b_im, c_re, c_im, with_ctx):
    ar, ai, bbr, bbi = s5_discretize(lam_re, lam_im, log_dt, b_re, b_im)

    def drive(u):
        u = jnp.flip(u, 1) if reverse else u
        return jnp.einsum('blgj,gpj->blgp', u, bbr), jnp.einsum('blgj,gpj->blgp', u, bbi)

    def readout(hr, hi):
        y = jnp.einsum('blgp,gjp->blgj', hr, c_re) - jnp.einsum('blgp,gjp->blgj', hi, c_im)
        return jnp.flip(y, 1) if reverse else y

    zero = jnp.zeros((uc.shape[0], SSM_GROUPS, SSM_STATE), jnp.float32)
    hcr, hci = complex_scan(ar, ai, *drive(uc), zero, zero)
    hlr, hli = complex_scan(ar, ai, *drive(ul), hcr[:, -1], hci[:, -1])
    y_c = readout(hcr, hci) if with_ctx else None
    return y_c, readout(hlr, hli)


def s5_branch(u_c, u_l, lam_re, lam_im, log_dt, b_re, b_im, c_re, c_im, d_skip, w_glu, b_glu, with_ctx):
    bsz = u_l.shape[0]
    uc = u_c.astype(jnp.float32).reshape(bsz, -1, SSM_GROUPS, SSM_GROUP)
    ul = u_l.astype(jnp.float32).reshape(bsz, -1, SSM_GROUPS, SSM_GROUP)
    fwd = s5_direction(uc, ul, False, lam_re[0], lam_im[0], log_dt[0], b_re[0], b_im[0], c_re[0], c_im[0], with_ctx)
    bwd = s5_direction(uc, ul, True, lam_re[1], lam_im[1], log_dt[1], b_re[1], b_im[1], c_re[1], c_im[1], with_ctx)

    def finish(y_f, y_b, u, dtype):
        y = (y_f + y_b).reshape(bsz, -1, SSM_W) + d_skip * u.reshape(bsz, -1, SSM_W)
        z = jax.nn.gelu(y)
        return (z * jax.nn.sigmoid(z @ w_glu + b_glu)).astype(dtype)

    y_l = finish(fwd[1], bwd[1], ul, u_l.dtype)
    y_c = finish(fwd[0], bwd[0], uc, u_c.dtype) if with_ctx else None
    return y_c, y_l


def mla_kv(kv_lat, k_rope, kv_norm, w_ukv, qk_k, rope):
    bsz, n, _ = kv_lat.shape
    kv = (rms_norm(kv_lat, kv_norm) @ w_ukv).reshape(bsz, n, MLA_HEADS, MLA_NOPE + MLA_V)
    k_r = jnp.broadcast_to(k_rope[:, :, None, :], (bsz, n, MLA_HEADS, MLA_ROPE))
    k = rms_norm(jnp.concatenate([kv[..., :MLA_NOPE], k_r], -1), qk_k)
    if rope is not None:
        k = apply_rope_part(k, rope)
    return k, kv[..., MLA_NOPE:]


def mla_q(q_lat, q_norm, w_uq, qk_q, rope):
    bsz, n, _ = q_lat.shape
    q = rms_norm((rms_norm(q_lat, q_norm) @ w_uq).reshape(bsz, n, MLA_HEADS, MLA_QK), qk_q)
    if rope is not None:
        q = apply_rope_part(q, rope)
    return q


def attend(q, k, v):
    s = jnp.einsum('bqhd,bkhd->bhqk', q, k, preferred_element_type=jnp.float32) * (MLA_QK ** -0.5)
    p = jax.nn.softmax(s, axis=-1)
    return jnp.einsum('bhqk,bkhd->bqhd', p.astype(v.dtype), v)


def latent_attention(q, k_all, v_all):
    bsz, n, h, dk = q.shape
    qb = q.reshape(bsz, n // Q_BLOCK, Q_BLOCK, h, dk).transpose(1, 0, 2, 3, 4)
    out = lax.map(lambda qi: attend(qi, k_all, v_all), qb)
    return out.transpose(1, 0, 2, 3, 4).reshape(bsz, n, h * MLA_V)


def short_conv(u, w, b):
    n = u.shape[1]
    up = jnp.pad(u, ((0, 0), (1, 1), (0, 0)))
    return w[0] * up[:, :n] + w[1] * up[:, 1:n + 1] + w[2] * up[:, 2:] + b


def hyena_filters(n, w1, b1, w2, b2, w3, freq, decay):
    t = jnp.arange(n, dtype=jnp.float32)
    t_norm = t / (n - 1)
    bands = jnp.linspace(1e-4, HY_BANDS - 1, HY_BANDS, dtype=jnp.float32)
    ang = (2.0 * math.pi * t / n)[:, None] * bands[None, :]
    feat = jnp.concatenate([t_norm[:, None], jnp.cos(ang), -jnp.sin(ang)], -1)
    h = jnp.sin(freq * (feat @ w1 + b1))
    h = jnp.sin(freq * (h @ w2 + b2))
    h = (h @ w3).reshape(n, HY_ORDER, 2, HY_W).astype(jnp.float32)
    h = h * jnp.exp(-t_norm[:, None, None, None] * jnp.abs(decay.astype(jnp.float32))[None])
    taps = jnp.concatenate([h[:, :, 0], jnp.zeros((1, HY_ORDER, HY_W), jnp.float32),
                            jnp.flip(h[1:, :, 1], 0)], 0)
    taps = taps / (jnp.sum(jnp.abs(taps), 0, keepdims=True) + EPS)
    return jnp.fft.rfft(taps, axis=0)


def long_conv(u, filt_f):
    n = u.shape[1]
    uf = jnp.fft.rfft(u, n=2 * n, axis=1)
    return jnp.fft.irfft(uf * filt_f[None], n=2 * n, axis=1)[:, :n]


def hyena_branch(u, conv_w, conv_b, w1, b1, w2, b2, w3, freq, decay, skip):
    n = u.shape[1]
    filt_f = hyena_filters(n, w1, b1, w2, b2, w3, freq, decay)
    us = short_conv(u, conv_w, conv_b).astype(jnp.float32)
    v, x1, x2 = jnp.split(us, 3, -1)
    skip = skip.astype(jnp.float32)
    z = x1 * (long_conv(v, filt_f[:, 0]) + skip[0] * v)
    y = x2 * (long_conv(z, filt_f[:, 1]) + skip[1] * z)
    return y.astype(u.dtype)


def merge_branches(gates, y_ssm, y_mla, y_hy, w_branch, w_out):
    g = jax.nn.sigmoid(gates).reshape(*gates.shape[:-1], N_BRANCH, D_MODEL)
    m = (g[..., 0, :] * (y_ssm @ w_branch[0]) + g[..., 1, :] * (y_mla @ w_branch[1])
         + g[..., 2, :] * (y_hy @ w_branch[2]))
    return m @ w_out


def route(h, router_w, router_bias):
    s = jax.nn.sigmoid(jnp.einsum('bld,de->ble', h, router_w, preferred_element_type=jnp.float32))
    sel = s + router_bias.astype(jnp.float32)
    grp = sel.reshape(*sel.shape[:-1], N_EXPERT_GROUPS, EXPERTS_PER_GROUP)
    g_idx = jnp.argmax(lax.top_k(grp, TOP_K)[0].sum(-1), -1)
    in_group = (jnp.arange(N_EXPERTS) // EXPERTS_PER_GROUP) == g_idx[..., None]
    _, e_idx = lax.top_k(jnp.where(in_group, sel, -jnp.inf), TOP_K)
    w = jnp.take_along_axis(s, e_idx, -1)
    w = w / jnp.sum(w, -1, keepdims=True)
    return jnp.sum(jax.nn.one_hot(e_idx, N_EXPERTS, dtype=jnp.float32) * w[..., None], -2)


def moe_ffn(h, comb, w_gate, w_up, w_down):
    out = jnp.zeros(h.shape, jnp.float32)
    for e in range(N_EXPERTS):
        he = jax.nn.silu(h @ w_gate[e]) * (h @ w_up[e])
        out = out + comb[..., e:e + 1] * (he @ w_down[e])
    return out.astype(h.dtype)


def setup_inputs(seed: int = 0) -> dict:
    key = jax.random.key(seed)
    ks = iter(jax.random.split(key, 48))

    def nrm(shape, scale):
        return scale * jax.random.normal(next(ks), shape, jnp.float32)

    def gain(shape):
        return 1.0 + nrm(shape, 0.02)

    ssm_shape = (DEPTH, 2, SSM_GROUPS, SSM_STATE)
    lam_im = jnp.broadcast_to(math.pi * jnp.arange(SSM_STATE, dtype=jnp.float32), ssm_shape) + nrm(ssm_shape, 0.01)
    log_dt = jax.random.uniform(next(ks), (DEPTH, 2, SSM_GROUPS), jnp.float32, math.log(1e-3), math.log(1e-1))
    decay_base = jnp.linspace(HY_DECAY_MIN, HY_DECAY_MAX, HY_W, dtype=jnp.float32)
    return {
        "x": nrm((BATCH, SEQ, D_MODEL), 1.0),
        "c": nrm((BATCH, D_MODEL), 1.0),
        "ctx": nrm((BATCH, CTX_LEN, D_MODEL), 1.0),
        "c_ctx": nrm((D_MODEL,), 1.0),
        "w_mod": nrm((DEPTH, D_MODEL, 6 * D_MODEL), 0.5 * D_MODEL ** -0.5),
        "b_mod": nrm((DEPTH, 6 * D_MODEL), 0.02),
        "norm1": gain((DEPTH, D_MODEL)),
        "norm2": gain((DEPTH, D_MODEL)),
        "w_in": nrm((DEPTH, D_MODEL, IN_COLS), D_MODEL ** -0.5),
        "ssm_lam_re": -0.5 + nrm(ssm_shape, 0.01),
        "ssm_lam_im": lam_im,
        "ssm_log_dt": log_dt,
        "ssm_b_re": nrm((DEPTH, 2, SSM_GROUPS, SSM_STATE, SSM_GROUP), (2 * SSM_GROUP) ** -0.5),
        "ssm_b_im": nrm((DEPTH, 2, SSM_GROUPS, SSM_STATE, SSM_GROUP), (2 * SSM_GROUP) ** -0.5),
        "ssm_c_re": nrm((DEPTH, 2, SSM_GROUPS, SSM_GROUP, SSM_STATE), SSM_STATE ** -0.5),
        "ssm_c_im": nrm((DEPTH, 2, SSM_GROUPS, SSM_GROUP, SSM_STATE), SSM_STATE ** -0.5),
        "ssm_d": nrm((DEPTH, SSM_W), 1.0),
        "ssm_w_glu": nrm((DEPTH, SSM_W, SSM_W), SSM_W ** -0.5),
        "ssm_b_glu": nrm((DEPTH, SSM_W), 0.02),
        "mla_q_norm": gain((DEPTH, MLA_Q_RANK)),
        "mla_w_uq": nrm((DEPTH, MLA_Q_RANK, MLA_HEADS * MLA_QK), MLA_Q_RANK ** -0.5),
        "mla_kv_norm": gain((DEPTH, MLA_KV_RANK)),
        "mla_w_ukv": nrm((DEPTH, MLA_KV_RANK, MLA_HEADS * (MLA_NOPE + MLA_V)), MLA_KV_RANK ** -0.5),
        "qk_norm_q": gain((DEPTH, MLA_QK)),
        "qk_norm_k": gain((DEPTH, MLA_QK)),
        "hy_conv_w": nrm((DEPTH, HY_CONV, 3 * HY_W), HY_CONV ** -0.5),
        "hy_conv_b": nrm((DEPTH, 3 * HY_W), 0.02),
        "hy_w1": nrm((DEPTH, HY_EMB, HY_HIDDEN), HY_EMB ** -0.5),
        "hy_b1": nrm((DEPTH, HY_HIDDEN), 0.02),
        "hy_w2": nrm((DEPTH, HY_HIDDEN, HY_HIDDEN), HY_HIDDEN ** -0.5),
        "hy_b2": nrm((DEPTH, HY_HIDDEN), 0.02),
        "hy_w3": nrm((DEPTH, HY_HIDDEN, HY_ORDER * 2 * HY_W), HY_HIDDEN ** -0.5),
        "hy_freq": gain((DEPTH, HY_HIDDEN)),
        "hy_decay": decay_base * (1.0 + nrm((DEPTH, HY_ORDER, 2, HY_W), 0.05)),
        "hy_skip": nrm((DEPTH, HY_ORDER, HY_W), 1.0),
        "w_branch": nrm((DEPTH, N_BRANCH, SSM_W, D_MODEL), SSM_W ** -0.5),
        "w_out": nrm((DEPTH, D_MODEL, D_MODEL), D_MODEL ** -0.5),
        "router_w": nrm((D_MODEL, N_EXPERTS), D_MODEL ** -0.5),
        "router_bias": nrm((N_EXPERTS,), 0.01),
        "moe_w_gate": nrm((DEPTH, N_EXPERTS, D_MODEL, D_EXPERT), D_MODEL ** -0.5),
        "moe_w_up": nrm((DEPTH, N_EXPERTS, D_MODEL, D_EXPERT), D_MODEL ** -0.5),
        "moe_w_down": nrm((DEPTH, N_EXPERTS, D_EXPERT, D_MODEL), D_EXPERT ** -0.5),
    }


def reference(x, c, ctx, c_ctx, w_mod, b_mod, norm1, norm2, w_in,
              ssm_lam_re, ssm_lam_im, ssm_log_dt, ssm_b_re, ssm_b_im, ssm_c_re, ssm_c_im,
              ssm_d, ssm_w_glu, ssm_b_glu,
              mla_q_norm, mla_w_uq, mla_kv_norm, mla_w_ukv, qk_norm_q, qk_norm_k,
              hy_conv_w, hy_conv_b, hy_w1, hy_b1, hy_w2, hy_b2, hy_w3, hy_freq, hy_decay, hy_skip,
              w_branch, w_out, router_w, router_bias, moe_w_gate, moe_w_up, moe_w_down):
    bsz, n_lat, _ = x.shape
    rope = axial_rope_tables(n_lat)
    silu_c = jax.nn.silu(c)[:, None, :]
    silu_cc = jax.nn.silu(c_ctx)[None, None, :]
    split_at = [int(s) for s in np.cumsum(IN_SPLITS)]
    xl, xc = x, ctx
    for l in range(DEPTH):
        ctx_out = l < DEPTH - 1
        ml = jnp.split(silu_c @ w_mod[l] + b_mod[l], 6, -1)
        mc = jnp.split(silu_cc @ w_mod[l] + b_mod[l], 6, -1)

        pl = jnp.split(modulate(xl, norm1[l], ml[0], ml[1]) @ w_in[l], split_at, -1)
        pc = jnp.split(modulate(xc, norm1[l], mc[0], mc[1]) @ w_in[l], split_at, -1)

        ssm_c, ssm_l = s5_branch(pc[0], pl[0], ssm_lam_re[l], ssm_lam_im[l], ssm_log_dt[l],
                                 ssm_b_re[l], ssm_b_im[l], ssm_c_re[l], ssm_c_im[l],
                                 ssm_d[l], ssm_w_glu[l], ssm_b_glu[l], ctx_out)

        kc, vc = mla_kv(pc[2], pc[3], mla_kv_norm[l], mla_w_ukv[l], qk_norm_k[l], None)
        kl, vl = mla_kv(pl[2], pl[3], mla_kv_norm[l], mla_w_ukv[l], qk_norm_k[l], rope)
        ql = mla_q(pl[1], mla_q_norm[l], mla_w_uq[l], qk_norm_q[l], rope)
        att_l = latent_attention(ql, jnp.concatenate([kc, kl], 1), jnp.concatenate([vc, vl], 1))

        hy_l = hyena_branch(pl[4], hy_conv_w[l], hy_conv_b[l], hy_w1[l], hy_b1[l], hy_w2[l], hy_b2[l],
                            hy_w3[l], hy_freq[l], hy_decay[l], hy_skip[l])
        mix_l = merge_branches(pl[5], ssm_l, att_l, hy_l, w_branch[l], w_out[l])

        if ctx_out:
            qc = mla_q(pc[1], mla_q_norm[l], mla_w_uq[l], qk_norm_q[l], None)
            att_c = attend(qc, kc, vc).reshape(bsz, xc.shape[1], MLA_HEADS * MLA_V)
            hy_c = hyena_branch(pc[4], hy_conv_w[l], hy_conv_b[l], hy_w1[l], hy_b1[l], hy_w2[l], hy_b2[l],
                                hy_w3[l], hy_freq[l], hy_decay[l], hy_skip[l])
            xc = xc + mc[2] * merge_branches(pc[5], ssm_c, att_c, hy_c, w_branch[l], w_out[l])
        xl = xl + ml[2] * mix_l

        hl = modulate(xl, norm2[l], ml[3], ml[4])
        xl = xl + ml[5] * moe_ffn(hl, route(hl, router_w, router_bias), moe_w_gate[l], moe_w_up[l], moe_w_down[l])
        if ctx_out:
            hc = modulate(xc, norm2[l], mc[3], mc[4])
            xc = xc + mc[5] * moe_ffn(hc, route(hc, router_w, router_bias), moe_w_gate[l], moe_w_up[l], moe_w_down[l])
    return xl
```

```python
import functools
import math

import numpy as np
import jax
import jax.numpy as jnp
from jax import lax
from jax.experimental import pallas as pl
from jax.experimental.pallas import tpu as pltpu

D_MODEL = 1024
EPS = 1e-6
GRID_W = 64
SSM_W = 512
SSM_GROUP = 16
SSM_GROUPS = SSM_W // SSM_GROUP
SSM_STATE = 64
MLA_HEADS = 8
MLA_Q_RANK = 384
MLA_KV_RANK = 256
MLA_NOPE = 64
MLA_ROPE = 32
MLA_V = 64
MLA_QK = MLA_NOPE + MLA_ROPE
ROPE_THETA = 10000.0
HY_W = 512
HY_ORDER = 2
HY_BANDS = 16
HY_EMB = 1 + 2 * HY_BANDS
HY_HIDDEN = 64
N_BRANCH = 3
IN_SPLITS = (SSM_W, MLA_Q_RANK, MLA_KV_RANK, MLA_ROPE, 3 * HY_W)
N_EXPERTS = 16
N_EXPERT_GROUPS = 4
EXPERTS_PER_GROUP = N_EXPERTS // N_EXPERT_GROUPS
D_EXPERT = 512

V7X_LANES = 128
V7X_SUBLANES = 8
V7X_VMEM_LIMIT = 56 * 1024 * 1024

ROW_TILE = 256
MOE_TILE = 256
HEAD_PAD = 128

_F32 = jnp.float32
_BF16 = jnp.bfloat16


def _cparams(*sem):
    return pltpu.CompilerParams(dimension_semantics=sem, vmem_limit_bytes=V7X_VMEM_LIMIT)


def _silu(v):
    return v * jax.nn.sigmoid(v)


def _modulated_norm(x, g, shift, scale):
    y = x * lax.rsqrt(jnp.mean(x * x, -1, keepdims=True) + EPS)
    return (y * g) * (1.0 + scale) + shift


def _dot(a, b):
    return jnp.dot(a, b, preferred_element_type=_F32)


def _mod_kernel(c_ref, w_ref, b_ref, o_ref):
    s = _silu(c_ref[...])
    o_ref[...] = jnp.dot(s, w_ref[...], preferred_element_type=_F32, precision=lax.Precision.HIGHEST) + b_ref[...]


def modulation_table(c, c_ctx, w_mod, b_mod):
    depth = w_mod.shape[0]
    bsz = c.shape[0]
    rows = V7X_SUBLANES * pl.cdiv(bsz + 1, V7X_SUBLANES)
    cvec = jnp.concatenate([c, c_ctx[None, :], jnp.zeros((rows - bsz - 1, D_MODEL), _F32)], 0)
    out = pl.pallas_call(
        _mod_kernel,
        grid=(depth, 6),
        in_specs=[
            pl.BlockSpec((rows, D_MODEL), lambda l, j: (0, 0)),
            pl.BlockSpec((None, D_MODEL, D_MODEL), lambda l, j: (l, 0, j)),
            pl.BlockSpec((None, 1, D_MODEL), lambda l, j: (l, 0, j)),
        ],
        out_specs=pl.BlockSpec((None, rows, D_MODEL), lambda l, j: (l, 0, j)),
        out_shape=jax.ShapeDtypeStruct((depth, rows, 6 * D_MODEL), _F32),
        compiler_params=_cparams("parallel", "parallel"),
        name="modulation",
    )(cvec, w_mod, b_mod.reshape(depth, 1, 6 * D_MODEL))
    m = out.reshape(depth, rows, 6, D_MODEL)
    lat = m[:, :bsz]
    ctx = jnp.broadcast_to(m[:, bsz:bsz + 1], lat.shape)
    tab = jnp.stack([ctx, lat], 2)
    return jnp.pad(tab, ((0, 0), (0, 0), (0, 0), (0, V7X_SUBLANES - 6), (0, 0)))


def _mod_spec(layer, n_ctx_tiles):
    return pl.BlockSpec((None, None, None, V7X_SUBLANES, D_MODEL),
                        lambda b, i: (layer, b, jnp.where(i >= n_ctx_tiles, 1, 0), 0, 0))


_IN_OFF = (0, SSM_W, SSM_W + MLA_Q_RANK, SSM_W + MLA_Q_RANK + MLA_KV_RANK)
_IN_HY = _IN_OFF[3]
_IN_ROPE = _IN_HY + 3 * HY_W
_IN_USED = _IN_ROPE + MLA_ROPE
_IN_GATES = sum(IN_SPLITS)


def pack_w_in(w_in):
    a = SSM_W + MLA_Q_RANK + MLA_KV_RANK
    return jnp.concatenate([w_in[:, :, :a], w_in[:, :, a + MLA_ROPE:_IN_GATES], w_in[:, :, a:a + MLA_ROPE]],
                           -1).astype(_BF16)


def split_router(router_w, router_bias):
    rw = router_w.T
    hi = rw.astype(_BF16)
    lo = (rw - hi.astype(_F32)).astype(_BF16)
    return hi, lo, router_bias.reshape(N_EXPERTS, 1).astype(_F32)


def _inproj_kernel(x_ref, mod_ref, g_ref, w_ref, u_ref, q_ref, kv_ref, hy_ref, kr_ref):
    xn = _modulated_norm(x_ref[...], g_ref[...], mod_ref[0:1, :], mod_ref[1:2, :]).astype(_BF16)
    u_ref[...] = _dot(xn, w_ref[:, _IN_OFF[0]:_IN_OFF[1]]).astype(u_ref.dtype)
    q_ref[...] = _dot(xn, w_ref[:, _IN_OFF[1]:_IN_OFF[2]]).astype(q_ref.dtype)
    kv_ref[...] = _dot(xn, w_ref[:, _IN_OFF[2]:_IN_OFF[3]]).astype(kv_ref.dtype)
    hy_ref[...] = _dot(xn, w_ref[:, _IN_HY:_IN_ROPE]).astype(hy_ref.dtype)
    kr_ref[...] = _dot(xn, w_ref[:, _IN_ROPE:_IN_USED]).astype(kr_ref.dtype)


def input_projection(x, mod_tab, norm1, w_in_used, layer, n_ctx_tiles):
    bsz, t, _ = x.shape
    widths = (SSM_W, MLA_Q_RANK, MLA_KV_RANK, 3 * HY_W, MLA_ROPE)
    row = lambda w: pl.BlockSpec((None, ROW_TILE, w), lambda b, i: (b, i, 0))
    return pl.pallas_call(
        _inproj_kernel,
        grid=(bsz, t // ROW_TILE),
        in_specs=[
            row(D_MODEL),
            _mod_spec(layer, n_ctx_tiles),
            pl.BlockSpec((None, 1, D_MODEL), lambda b, i: (layer, 0, 0)),
            pl.BlockSpec((None, D_MODEL, _IN_USED), lambda b, i: (layer, 0, 0)),
        ],
        out_specs=[row(w) for w in widths],
        out_shape=[jax.ShapeDtypeStruct((bsz, t, w), _BF16) for w in widths],
        compiler_params=_cparams("parallel", "parallel"),
        name="input_projection",
    )(x, mod_tab, norm1, w_in_used)


def _max4(a, b, c, d):
    hi1, lo1 = jnp.maximum(a, b), jnp.minimum(a, b)
    hi2, lo2 = jnp.maximum(c, d), jnp.minimum(c, d)
    return jnp.maximum(hi1, hi2), jnp.maximum(jnp.minimum(hi1, hi2), jnp.maximum(lo1, lo2))


def _first_argmax(vals):
    best, idx = vals[0], jnp.zeros(vals[0].shape, jnp.int32)
    for j in range(1, len(vals)):
        upd = vals[j] > best
        idx = jnp.where(upd, j, idx)
        best = jnp.where(upd, vals[j], best)
    return idx


def _pick(vals, idx):
    out = vals[-1]
    for j in range(len(vals) - 2, -1, -1):
        out = jnp.where(idx == j, vals[j], out)
    return out


def _route_rows(logits_t, bias_col):
    s = jax.nn.sigmoid(logits_t)
    sel = s + bias_col
    s_rows = [s[e:e + 1, :] for e in range(N_EXPERTS)]
    sel_rows = [sel[e:e + 1, :] for e in range(N_EXPERTS)]
    gsum = []
    for g in range(N_EXPERT_GROUPS):
        top, second = _max4(*sel_rows[EXPERTS_PER_GROUP * g:EXPERTS_PER_GROUP * (g + 1)])
        gsum.append(top + second)
    gidx = _first_argmax(gsum)
    v = [_pick([sel_rows[EXPERTS_PER_GROUP * g + j] for g in range(N_EXPERT_GROUPS)], gidx)
         for j in range(EXPERTS_PER_GROUP)]
    u = [_pick([s_rows[EXPERTS_PER_GROUP * g + j] for g in range(N_EXPERT_GROUPS)], gidx)
         for j in range(EXPERTS_PER_GROUP)]
    i1 = _first_argmax(v)
    neg = jnp.full(v[0].shape, -jnp.inf, _F32)
    i2 = _first_argmax([jnp.where(i1 == j, neg, v[j]) for j in range(EXPERTS_PER_GROUP)])
    wa, wb = _pick(u, i1), _pick(u, i2)
    tot = wa + wb
    e0 = (EXPERTS_PER_GROUP * gidx + i1).astype(_F32)
    e1 = (EXPERTS_PER_GROUP * gidx + i2).astype(_F32)
    return e0, e1, wa / tot, wb / tot


def _merge_kernel(x_ref, ys_ref, ya_ref, yh_ref, mod_ref, g1_ref, g2_ref, wg_ref, wb_ref, wo_ref,
                  rwh_ref, rwl_ref, rb_ref, xo_ref, hl_ref, rt_ref):
    x = x_ref[...]
    xn = _modulated_norm(x, g1_ref[...], mod_ref[0:1, :], mod_ref[1:2, :]).astype(_BF16)
    m = None
    for k, y_ref in enumerate((ys_ref, ya_ref, yh_ref)):
        gate = jax.nn.sigmoid(_dot(xn, wg_ref[:, k * D_MODEL:(k + 1) * D_MODEL]))
        term = gate * _dot(y_ref[...], wb_ref[k])
        m = term if m is None else m + term
    x_new = x + mod_ref[2:3, :] * _dot(m.astype(_BF16), wo_ref[...])
    xo_ref[...] = x_new
    hl = _modulated_norm(x_new, g2_ref[...], mod_ref[3:4, :], mod_ref[4:5, :])
    hl_hi = hl.astype(_BF16)
    hl_ref[...] = hl_hi
    hl_lo = (hl - hl_hi.astype(_F32)).astype(_BF16)
    nt = (((1,), (1,)), ((), ()))
    logits_t = (lax.dot_general(rwh_ref[...], hl_hi, nt, preferred_element_type=_F32)
                + lax.dot_general(rwl_ref[...], hl_hi, nt, preferred_element_type=_F32)
                + lax.dot_general(rwh_ref[...], hl_lo, nt, preferred_element_type=_F32))
    e0, e1, w0, w1 = _route_rows(logits_t, rb_ref[...])
    zero = jnp.zeros_like(w0)
    rt_ref[...] = jnp.concatenate([e0, e1, w0, w1, zero, zero, zero, zero], 0)


def merge_and_route(x, y_ssm, y_att, y_hy, mod_tab, norm1, norm2, w_gates, w_branch, w_out,
                    rw_hi, rw_lo, rb_col, layer, n_ctx_tiles):
    bsz, t, _ = x.shape
    row = lambda w: pl.BlockSpec((None, ROW_TILE, w), lambda b, i: (b, i, 0))
    whole = lambda *shape: pl.BlockSpec(shape, lambda b, i: (0,) * len(shape))
    return pl.pallas_call(
        _merge_kernel,
        grid=(bsz, t // ROW_TILE),
        in_specs=[
            row(D_MODEL), row(SSM_W), row(MLA_HEADS * MLA_V), row(HY_W),
            _mod_spec(layer, n_ctx_tiles),
            pl.BlockSpec((None, 1, D_MODEL), lambda b, i: (layer, 0, 0)),
            pl.BlockSpec((None, 1, D_MODEL), lambda b, i: (layer, 0, 0)),
            pl.BlockSpec((None, D_MODEL, N_BRANCH * D_MODEL), lambda b, i: (layer, 0, 0)),
            pl.BlockSpec((None, N_BRANCH, SSM_W, D_MODEL), lambda b, i: (layer, 0, 0, 0)),
            pl.BlockSpec((None, D_MODEL, D_MODEL), lambda b, i: (layer, 0, 0)),
            whole(N_EXPERTS, D_MODEL), whole(N_EXPERTS, D_MODEL), whole(N_EXPERTS, 1),
        ],
        out_specs=[row(D_MODEL), row(D_MODEL),
                   pl.BlockSpec((None, V7X_SUBLANES, ROW_TILE), lambda b, i: (b, 0, i))],
        out_shape=[jax.ShapeDtypeStruct((bsz, t, D_MODEL), _F32),
                   jax.ShapeDtypeStruct((bsz, t, D_MODEL), _BF16),
                   jax.ShapeDtypeStruct((bsz, V7X_SUBLANES, t), _F32)],
        compiler_params=_cparams("parallel", "parallel"),
        name="merge_and_route",
    )(x, y_ssm, y_att, y_hy, mod_tab, norm1, norm2, w_gates, w_branch, w_out, rw_hi, rw_lo, rb_col)


def _moe_kernel(te_ref, tv_ref, xs_ref, wrow_ref, wg_ref, wu_ref, wd_ref, o_ref):
    del te_ref

    @pl.when(tv_ref[pl.program_id(0)] > 0)
    def _():
        h = xs_ref[...]
        act = (_silu(_dot(h, wg_ref[...])) * _dot(h, wu_ref[...])).astype(_BF16)
        y = _dot(act, wd_ref[...])
        wrow = wrow_ref[...]
        for j in range(D_MODEL // V7X_LANES):
            sl = slice(j * V7X_LANES, (j + 1) * V7X_LANES)
            o_ref[:, sl] = (y[:, sl] * wrow).astype(o_ref.dtype)

    @pl.when(tv_ref[pl.program_id(0)] == 0)
    def _():
        o_ref[...] = jnp.zeros_like(o_ref)


def grouped_experts(xs, wrow, tile_expert, tile_valid, w_gate, w_up, w_down, layer):
    rows = xs.shape[0]
    n_tiles = rows // MOE_TILE
    wspec = lambda a, b: pl.BlockSpec((None, None, a, b), lambda i, te, tv: (layer, te[i], 0, 0))
    return pl.pallas_call(
        _moe_kernel,
        grid_spec=pltpu.PrefetchScalarGridSpec(
            num_scalar_prefetch=2,
            grid=(n_tiles,),
            in_specs=[
                pl.BlockSpec((MOE_TILE, D_MODEL), lambda i, te, tv: (i, 0)),
                pl.BlockSpec((MOE_TILE, V7X_LANES), lambda i, te, tv: (i, 0)),
                wspec(D_MODEL, D_EXPERT), wspec(D_MODEL, D_EXPERT), wspec(D_EXPERT, D_MODEL),
            ],
            out_specs=pl.BlockSpec((MOE_TILE, D_MODEL), lambda i, te, tv: (i, 0)),
        ),
        out_shape=jax.ShapeDtypeStruct((rows, D_MODEL), _BF16),
        compiler_params=_cparams("arbitrary"),
        name="grouped_experts",
    )(tile_expert, tile_valid, xs, wrow, w_gate, w_up, w_down)


def expert_dispatch_plan(route):
    bsz, _, t = route.shape
    n = bsz * t
    e = jnp.stack([route[:, 0, :], route[:, 1, :]], -1).reshape(n * 2).astype(jnp.int32)
    w = jnp.stack([route[:, 2, :], route[:, 3, :]], -1).reshape(n * 2)
    onehot = (e[:, None] == jnp.arange(N_EXPERTS, dtype=jnp.int32)[None, :]).astype(jnp.int32)
    rank = jnp.take_along_axis(jnp.cumsum(onehot, 0) - onehot, e[:, None], 1)[:, 0]
    count = jnp.sum(onehot, 0)
    tiles = (count + MOE_TILE - 1) // MOE_TILE
    tile_start = jnp.cumsum(tiles) - tiles
    dest = tile_start[e] * MOE_TILE + rank
    rows = 2 * n + N_EXPERTS * MOE_TILE
    n_tiles = rows // MOE_TILE
    tile_ids = jnp.arange(n_tiles, dtype=jnp.int32)
    tile_expert = jnp.clip(jnp.searchsorted(jnp.cumsum(tiles), tile_ids, side="right"), 0, N_EXPERTS - 1)
    tile_valid = (tile_ids < jnp.sum(tiles)).astype(jnp.int32)
    src = jnp.zeros((rows,), jnp.int32).at[dest].set(jnp.arange(2 * n, dtype=jnp.int32) // 2)
    wsorted = jnp.zeros((rows,), _F32).at[dest].set(w)
    wrow = jnp.broadcast_to(wsorted[:, None], (rows, V7X_LANES))
    dest2 = dest.reshape(n, 2)
    return src, wrow, tile_expert.astype(jnp.int32), tile_valid, dest2[:, 0], dest2[:, 1]


def _combine_kernel(x_ref, a_ref, b_ref, mod_ref, o_ref):
    o_ref[...] = x_ref[...] + mod_ref[5:6, :] * (a_ref[...].astype(_F32) + b_ref[...].astype(_F32))


def combine_experts(x, ya, yb, mod_tab, layer, n_ctx_tiles):
    bsz, t, _ = x.shape
    row = pl.BlockSpec((None, ROW_TILE, D_MODEL), lambda b, i: (b, i, 0))
    return pl.pallas_call(
        _combine_kernel,
        grid=(bsz, t // ROW_TILE),
        in_specs=[row, row, row, _mod_spec(layer, n_ctx_tiles)],
        out_specs=row,
        out_shape=jax.ShapeDtypeStruct(x.shape, _F32),
        compiler_params=_cparams("parallel", "parallel"),
        name="combine_experts",
    )(x, ya, yb, mod_tab)


def moe_block(x_new, hl, route, mod_tab, w_gate, w_up, w_down, layer, n_ctx_tiles):
    bsz, t, _ = x_new.shape
    src, wrow, tile_expert, tile_valid, d0, d1 = expert_dispatch_plan(route)
    xs = jnp.take(hl.reshape(bsz * t, D_MODEL), src, axis=0)
    ys = grouped_experts(xs, wrow, tile_expert, tile_valid, w_gate, w_up, w_down, layer)
    ya = jnp.take(ys, d0, axis=0).reshape(bsz, t, D_MODEL)
    yb = jnp.take(ys, d1, axis=0).reshape(bsz, t, D_MODEL)
    return combine_experts(x_new, ya, yb, mod_tab, layer, n_ctx_tiles)


ATT_TILE = 256
_ROPE_LO = MLA_NOPE
_ROPE_HALF = MLA_ROPE // 2


def rope_tables(n_ctx, n_lat):
    rows = n_lat // GRID_W
    row = jnp.broadcast_to(jnp.arange(rows, dtype=_F32)[:, None], (rows, GRID_W)).reshape(n_lat)
    col = jnp.broadcast_to(jnp.arange(GRID_W, dtype=_F32)[None, :], (rows, GRID_W)).reshape(n_lat)
    n_f = MLA_ROPE // 4
    inv = ROPE_THETA ** (-jnp.arange(n_f, dtype=_F32) / n_f)
    ang = jnp.concatenate([row[:, None] * inv, col[:, None] * inv], -1)
    cos = jnp.concatenate([jnp.ones((n_ctx, _ROPE_HALF), _F32), jnp.cos(ang)], 0)
    sin = jnp.concatenate([jnp.zeros((n_ctx, _ROPE_HALF), _F32), jnp.sin(ang)], 0)
    t = n_ctx + n_lat
    ones, zeros = jnp.ones((t, MLA_NOPE), _F32), jnp.zeros((t, MLA_NOPE), _F32)
    tail1, tail0 = jnp.ones((t, HEAD_PAD - MLA_QK), _F32), jnp.zeros((t, HEAD_PAD - MLA_QK), _F32)
    cos_p = jnp.concatenate([ones, cos, cos, tail1], -1)
    sin_p = jnp.concatenate([zeros, -sin, sin, tail0], -1)
    return cos.T, sin.T, cos_p, sin_p


def pack_mla_weights(w_uq, w_ukv, qk_norm_q, qk_norm_k):
    depth = w_uq.shape[0]
    pad = HEAD_PAD - MLA_QK
    wq = w_uq.reshape(depth, MLA_Q_RANK, MLA_HEADS, MLA_QK)
    wq = jnp.pad(wq, ((0, 0), (0, 0), (0, 0), (0, pad))).reshape(depth, MLA_Q_RANK, MLA_HEADS * HEAD_PAD)
    wq_t = jnp.swapaxes(wq, 1, 2).astype(_BF16)
    wkv = w_ukv.reshape(depth, MLA_KV_RANK, MLA_HEADS, MLA_NOPE + MLA_V)
    wk = jnp.pad(wkv[..., :MLA_NOPE], ((0, 0), (0, 0), (0, 0), (0, HEAD_PAD - MLA_NOPE)))
    wk = wk.reshape(depth, MLA_KV_RANK, MLA_HEADS * HEAD_PAD).astype(_BF16)
    wv_t = jnp.swapaxes(wkv[..., MLA_NOPE:].reshape(depth, MLA_KV_RANK, MLA_HEADS * MLA_V), 1, 2).astype(_BF16)
    place = jnp.zeros((MLA_ROPE, HEAD_PAD), _F32).at[jnp.arange(MLA_ROPE), _ROPE_LO + jnp.arange(MLA_ROPE)].set(1.0)
    place = jnp.tile(place, (1, MLA_HEADS)).astype(_BF16)
    gq = jnp.pad(qk_norm_q, ((0, 0), (0, pad)))
    gq_col = jnp.broadcast_to(gq[:, :, None], (depth, HEAD_PAD, ATT_TILE)).astype(_F32)
    gk_row = jnp.pad(qk_norm_k, ((0, 0), (0, pad))).reshape(depth, 1, HEAD_PAD).astype(_F32)
    return wq_t, wk, wv_t, place, gq_col, gk_row


def _mla_prep_kernel(ql_ref, kvl_ref, kr_ref, qn_ref, kvn_ref, wq_ref, wk_ref, wv_ref, place_ref, gq_ref, gk_ref,
                     cos_t_ref, sin_t_ref, cos_p_ref, sin_p_ref, qt_ref, k_ref, vt_ref):
    nt = (((1,), (1,)), ((), ()))
    ql = ql_ref[...].astype(_F32)
    qn = (ql * lax.rsqrt(jnp.mean(ql * ql, -1, keepdims=True) + EPS) * qn_ref[...]).astype(_BF16)
    kvl = kvl_ref[...].astype(_F32)
    kvn = (kvl * lax.rsqrt(jnp.mean(kvl * kvl, -1, keepdims=True) + EPS) * kvn_ref[...]).astype(_BF16)

    q_t = lax.dot_general(wq_ref[...], qn, nt, preferred_element_type=_F32)
    cos_t, sin_t = cos_t_ref[...], sin_t_ref[...]
    scale = MLA_QK ** -0.5
    lo, mid, hi = _ROPE_LO, _ROPE_LO + _ROPE_HALF, _ROPE_LO + MLA_ROPE
    for h in range(MLA_HEADS):
        blk = q_t[h * HEAD_PAD:(h + 1) * HEAD_PAD, :]
        ms = jnp.sum(blk * blk, 0, keepdims=True) * (1.0 / MLA_QK)
        y = blk * (lax.rsqrt(ms + EPS) * scale) * gq_ref[...]
        r1, r2 = y[lo:mid, :], y[mid:hi, :]
        rot = jnp.concatenate([y[:lo, :], r1 * cos_t - r2 * sin_t, r2 * cos_t + r1 * sin_t, y[hi:, :]], 0)
        qt_ref[h] = rot.astype(qt_ref.dtype)

    k_pre = _dot(kvn, wk_ref[...]) + _dot(kr_ref[...], place_ref[...])
    cos_p, sin_p = cos_p_ref[...], sin_p_ref[...]
    lane = lax.broadcasted_iota(jnp.int32, cos_p.shape, 1)
    for h in range(MLA_HEADS):
        blk = k_pre[:, h * HEAD_PAD:(h + 1) * HEAD_PAD]
        ms = jnp.sum(blk * blk, -1, keepdims=True) * (1.0 / MLA_QK)
        y = blk * lax.rsqrt(ms + EPS) * gk_ref[...]
        swap = jnp.where(lane < mid, pltpu.roll(y, HEAD_PAD - _ROPE_HALF, 1), pltpu.roll(y, _ROPE_HALF, 1))
        k_ref[h] = (y * cos_p + swap * sin_p).astype(k_ref.dtype)

    v_t = lax.dot_general(wv_ref[...], kvn, nt, preferred_element_type=_F32)
    for h in range(MLA_HEADS):
        vt_ref[h] = v_t[h * MLA_V:(h + 1) * MLA_V, :].astype(vt_ref.dtype)


def mla_prepare(q_lat, kv_lat, k_rope, q_norm, kv_norm, packed, tables, layer):
    bsz, t, _ = q_lat.shape
    wq_t, wk, wv_t, place, gq_col, gk_row = packed
    cos_t, sin_t, cos_p, sin_p = tables
    n_tiles = t // ATT_TILE
    row = lambda w: pl.BlockSpec((None, ATT_TILE, w), lambda b, i: (b, i, 0))
    lay = lambda *shape: pl.BlockSpec((None,) + shape, lambda b, i: (layer,) + (0,) * len(shape))
    return pl.pallas_call(
        _mla_prep_kernel,
        grid=(bsz, n_tiles),
        in_specs=[
            row(MLA_Q_RANK), row(MLA_KV_RANK), row(MLA_ROPE),
            lay(1, MLA_Q_RANK), lay(1, MLA_KV_RANK),
            lay(MLA_HEADS * HEAD_PAD, MLA_Q_RANK), lay(MLA_KV_RANK, MLA_HEADS * HEAD_PAD),
            lay(MLA_HEADS * MLA_V, MLA_KV_RANK),
            pl.BlockSpec((MLA_ROPE, MLA_HEADS * HEAD_PAD), lambda b, i: (0, 0)),
            lay(HEAD_PAD, ATT_TILE), lay(1, HEAD_PAD),
            pl.BlockSpec((_ROPE_HALF, ATT_TILE), lambda b, i: (0, i)),
            pl.BlockSpec((_ROPE_HALF, ATT_TILE), lambda b, i: (0, i)),
            pl.BlockSpec((ATT_TILE, HEAD_PAD), lambda b, i: (i, 0)),
            pl.BlockSpec((ATT_TILE, HEAD_PAD), lambda b, i: (i, 0)),
        ],
        out_specs=[
            pl.BlockSpec((None, MLA_HEADS, HEAD_PAD, ATT_TILE), lambda b, i: (b, 0, 0, i)),
            pl.BlockSpec((None, MLA_HEADS, ATT_TILE, HEAD_PAD), lambda b, i: (b, 0, i, 0)),
            pl.BlockSpec((None, MLA_HEADS, None, MLA_V, ATT_TILE), lambda b, i: (b, 0, i, 0, 0)),
        ],
        out_shape=[
            jax.ShapeDtypeStruct((bsz, MLA_HEADS, HEAD_PAD, t), _BF16),
            jax.ShapeDtypeStruct((bsz, MLA_HEADS, t, HEAD_PAD), _BF16),
            jax.ShapeDtypeStruct((bsz, MLA_HEADS, n_tiles, MLA_V, ATT_TILE), _BF16),
        ],
        compiler_params=_cparams("parallel", "parallel"),
        name="mla_prepare",
    )(q_lat, kv_lat, k_rope, q_norm, kv_norm, wq_t, wk, wv_t, place, gq_col, gk_row, cos_t, sin_t, cos_p, sin_p)


def _attention_kernel(qt_ref, k_ref, vt_ref, o_ref, *, n_ctx_tiles, n_tiles):
    i = pl.program_id(2)
    n_kv = jnp.where(i < n_ctx_tiles, n_ctx_tiles, n_tiles)
    q_t = qt_ref[...]

    def body(j, carry):
        m, l, acc = carry
        k = k_ref[pl.ds(pl.multiple_of(j * ATT_TILE, ATT_TILE), ATT_TILE), :]
        s_t = _dot(k, q_t)
        m_new = jnp.maximum(m, jnp.max(s_t, 0, keepdims=True))
        alpha = jnp.exp(m - m_new)
        p = jnp.exp(s_t - m_new)
        l = alpha * l + jnp.sum(p, 0, keepdims=True)
        acc = alpha * acc + _dot(vt_ref[j], p.astype(_BF16))
        return m_new, l, acc

    tq = q_t.shape[1]
    init = (jnp.full((1, tq), -jnp.inf, _F32), jnp.zeros((1, tq), _F32), jnp.zeros((MLA_V, tq), _F32))
    _, l, acc = lax.fori_loop(0, n_kv, body, init)
    o_ref[...] = (acc / l).astype(o_ref.dtype)


def flash_attention(q_t, k, v_t, n_ctx_tiles):
    bsz, heads, _, t = q_t.shape
    n_tiles = t // ATT_TILE
    return pl.pallas_call(
        functools.partial(_attention_kernel, n_ctx_tiles=n_ctx_tiles, n_tiles=n_tiles),
        grid=(bsz, heads, n_tiles),
        in_specs=[
            pl.BlockSpec((None, None, HEAD_PAD, ATT_TILE), lambda b, h, i: (b, h, 0, i)),
            pl.BlockSpec((None, None, t, HEAD_PAD), lambda b, h, i: (b, h, 0, 0)),
            pl.BlockSpec((None, None, n_tiles, MLA_V, ATT_TILE), lambda b, h, i: (b, h, 0, 0, 0)),
        ],
        out_specs=pl.BlockSpec((None, MLA_V, ATT_TILE), lambda b, h, i: (b, h, i)),
        out_shape=jax.ShapeDtypeStruct((bsz, heads * MLA_V, t), _BF16),
        compiler_params=_cparams("parallel", "parallel", "arbitrary"),
        name="flash_attention",
    )(q_t, k, v_t)


S5_KG = SSM_W // V7X_LANES
S5_GPK = SSM_GROUPS // S5_KG
S5_HALF = S5_GPK * SSM_STATE
S5_COLS = 2 * SSM_GROUPS * SSM_STATE
S5_PAIRS = SSM_GROUPS * SSM_STATE // V7X_LANES
S5_LEVELS = (1, 2, 4)


def _s5_disc_kernel(lr_ref, li_ref, ldt_ref, br_ref, bi_ref, pr_ref, pi_ref, bbr_ref, bbi_ref):
    lr = jnp.minimum(lr_ref[...], -1e-4)
    li = li_ref[...]
    dt = jnp.exp(ldt_ref[...])
    mag = jnp.exp(lr * dt)
    ar, ai = mag * jnp.cos(li * dt), mag * jnp.sin(li * dt)
    den = lr * lr + li * li
    gr = ((ar - 1.0) * lr + ai * li) / den
    gi = (ai * lr - (ar - 1.0) * li) / den
    br, bi = br_ref[...], bi_ref[...]
    bbr_ref[...] = gr * br - gi * bi
    bbi_ref[...] = gr * bi + gi * br
    pr, pi = ar, ai
    for k in range(V7X_SUBLANES):
        pr_ref[k] = pr
        pi_ref[k] = pi
        pr, pi = pr * ar - pi * ai, pr * ai + pi * ar


def s5_discretize(lam_re, lam_im, log_dt, b_re, b_im):
    depth = lam_re.shape[0]
    ld = depth * 2
    g, p, j = SSM_GROUPS, SSM_STATE, SSM_GROUP
    flat = lambda a: a.reshape((ld,) + a.shape[2:])
    g1p = lambda a: a.reshape(ld, g, 1, p)
    ldt = jnp.broadcast_to(flat(log_dt)[:, :, None, None], (ld, g, 1, p))
    b_t = lambda a: jnp.swapaxes(flat(a), 2, 3)
    gp = pl.BlockSpec((None, g, 1, p), lambda i: (i, 0, 0, 0))
    gjp = pl.BlockSpec((None, g, j, p), lambda i: (i, 0, 0, 0))
    pw = pl.BlockSpec((None, V7X_SUBLANES, g, 1, p), lambda i: (i, 0, 0, 0, 0))
    pow_r, pow_i, bbr, bbi = pl.pallas_call(
        _s5_disc_kernel,
        grid=(ld,),
        in_specs=[gp, gp, gp, gjp, gjp],
        out_specs=[pw, pw, gjp, gjp],
        out_shape=[jax.ShapeDtypeStruct((ld, V7X_SUBLANES, g, 1, p), _F32)] * 2
        + [jax.ShapeDtypeStruct((ld, g, j, p), _F32)] * 2,
        compiler_params=_cparams("parallel"),
        name="s5_discretize",
    )(g1p(lam_re), g1p(lam_im), ldt, b_t(b_re), b_t(b_im))
    return pow_r.reshape(ld, V7X_SUBLANES, g, p), pow_i.reshape(ld, V7X_SUBLANES, g, p), bbr, bbi


def pack_s5(pow_r, pow_i, bbr, bbi, c_re, c_im):
    ld = pow_r.shape[0]
    n = SSM_GROUPS * SSM_STATE
    rev = (jnp.arange(ld) % 2 == 1)[:, None, None]
    t = jnp.arange(V7X_SUBLANES)[None, :, None]
    pr, pi = pow_r.reshape(ld, V7X_SUBLANES, n), pow_i.reshape(ld, V7X_SUBLANES, n)
    consts = []
    for s in S5_LEVELS:
        keep = jnp.where(rev, t < V7X_SUBLANES - s, t >= s)
        for a in (pr, pi):
            consts.append(jnp.where(keep, a[:, s - 1:s, :], 0.0))
    for a in (pr, pi):
        consts.append(jnp.where(rev, a[:, ::-1, :], a))
    consts = jnp.stack(consts, 1)
    eye = jnp.eye(S5_GPK, dtype=_F32)
    bb = jnp.stack([bbr, bbi], 1).reshape(ld, 2, S5_KG, S5_GPK, SSM_GROUP, SSM_STATE)
    bd = jnp.einsum("dckgjp,gh->dkgjchp", bb, eye).reshape(ld, S5_KG, V7X_LANES, 2 * S5_HALF).astype(_BF16)
    cc = jnp.stack([c_re, -c_im], 2)
    cc = cc.reshape(ld, 2, S5_KG, S5_GPK, SSM_GROUP, SSM_STATE)
    cd = jnp.einsum("dckgjp,gh->dkcgphj", cc, eye).reshape(ld, S5_KG, 2 * S5_HALF, V7X_LANES).astype(_BF16)
    return consts, bd, cd


def _s5_scan_kernel(u_ref, bd_ref, cd_ref, k_ref, *rest, reverse, finish):
    if finish:
        yf_ref, d_ref, wg_ref, bg_ref, o_ref, bu_ref, carry_ref = rest
    else:
        o_ref, bu_ref, carry_ref = rest
    n_rows = u_ref.shape[0]

    @pl.when(pl.program_id(1) == 0)
    def _():
        carry_ref[...] = jnp.zeros_like(carry_ref)

    u = u_ref[...]
    for kg in range(S5_KG):
        bu_ref[:, kg * 2 * S5_HALF:(kg + 1) * 2 * S5_HALF] = _dot(u[:, kg * V7X_LANES:(kg + 1) * V7X_LANES], bd_ref[kg])

    n_vregs = n_rows // V7X_SUBLANES
    last = 0 if reverse else V7X_SUBLANES - 1

    def step(r, _):
        row = pl.multiple_of((n_vregs - 1 - r if reverse else r) * V7X_SUBLANES, V7X_SUBLANES)
        for jp in range(S5_PAIRS):
            kg, q = divmod(jp, S5_PAIRS // S5_KG)
            cr = kg * 2 * S5_HALF + q * V7X_LANES
            ci = cr + S5_HALF
            cl = jp * V7X_LANES
            xr = bu_ref[pl.ds(row, V7X_SUBLANES), cr:cr + V7X_LANES]
            xi = bu_ref[pl.ds(row, V7X_SUBLANES), ci:ci + V7X_LANES]
            for lvl, s in enumerate(S5_LEVELS):
                shift = V7X_SUBLANES - s if reverse else s
                sr, si = pltpu.roll(xr, shift, 0), pltpu.roll(xi, shift, 0)
                ar = k_ref[2 * lvl, :, cl:cl + V7X_LANES]
                ai = k_ref[2 * lvl + 1, :, cl:cl + V7X_LANES]
                xr, xi = xr + (ar * sr - ai * si), xi + (ar * si + ai * sr)
            pr = k_ref[2 * len(S5_LEVELS), :, cl:cl + V7X_LANES]
            pi = k_ref[2 * len(S5_LEVELS) + 1, :, cl:cl + V7X_LANES]
            hr0 = carry_ref[:, cr:cr + V7X_LANES]
            hi0 = carry_ref[:, ci:ci + V7X_LANES]
            xr, xi = xr + (pr * hr0 - pi * hi0), xi + (pr * hi0 + pi * hr0)
            bu_ref[pl.ds(row, V7X_SUBLANES), cr:cr + V7X_LANES] = xr
            bu_ref[pl.ds(row, V7X_SUBLANES), ci:ci + V7X_LANES] = xi
            carry_ref[:, cr:cr + V7X_LANES] = jnp.broadcast_to(xr[last:last + 1, :], xr.shape)
            carry_ref[:, ci:ci + V7X_LANES] = jnp.broadcast_to(xi[last:last + 1, :], xi.shape)
        return 0

    lax.fori_loop(0, n_vregs, step, 0)

    ys = [_dot(bu_ref[:, kg * 2 * S5_HALF:(kg + 1) * 2 * S5_HALF].astype(_BF16), cd_ref[kg]) for kg in range(S5_KG)]
    y = jnp.concatenate(ys, -1)
    if finish:
        y = y + yf_ref[...] + d_ref[...] * u.astype(_F32)
        z = jax.nn.gelu(y)
        o_ref[...] = (z * jax.nn.sigmoid(_dot(z.astype(_BF16), wg_ref[...]) + bg_ref[...])).astype(o_ref.dtype)
    else:
        o_ref[...] = y


def s5_scan(u, consts, bd, cd, layer, n_ctx_tiles, reverse, finish_args=None):
    bsz, t, _ = u.shape
    n_tiles = t // ROW_TILE
    d = 2 * layer + (1 if reverse else 0)
    if reverse:
        chunk = lambda i: jnp.where(i < n_ctx_tiles, n_ctx_tiles - 1 - i, n_tiles - 1 - (i - n_ctx_tiles))
    else:
        chunk = lambda i: i
    row = lambda w: pl.BlockSpec((None, ROW_TILE, w), lambda b, i: (b, chunk(i), 0))
    lay = lambda *shape: pl.BlockSpec((None,) + shape, lambda b, i: (d,) + (0,) * len(shape))
    in_specs = [row(SSM_W), lay(S5_KG, V7X_LANES, 2 * S5_HALF), lay(S5_KG, 2 * S5_HALF, V7X_LANES),
                lay(2 * len(S5_LEVELS) + 2, V7X_SUBLANES, SSM_GROUPS * SSM_STATE)]
    args = [u, bd, cd, consts]
    finish = finish_args is not None
    if finish:
        y_fwd, d_skip, w_glu, b_glu = finish_args
        lyr = lambda *shape: pl.BlockSpec((None,) + shape, lambda b, i: (layer,) + (0,) * len(shape))
        in_specs += [row(SSM_W), lyr(1, SSM_W), lyr(SSM_W, SSM_W), lyr(1, SSM_W)]
        args += [y_fwd, d_skip, w_glu, b_glu]
    return pl.pallas_call(
        functools.partial(_s5_scan_kernel, reverse=reverse, finish=finish),
        grid=(bsz, n_tiles),
        in_specs=in_specs,
        out_specs=row(SSM_W),
        out_shape=jax.ShapeDtypeStruct((bsz, t, SSM_W), _BF16 if finish else _F32),
        scratch_shapes=[pltpu.VMEM((ROW_TILE, S5_COLS), _F32), pltpu.VMEM((V7X_SUBLANES, S5_COLS), _F32)],
        compiler_params=_cparams("parallel", "arbitrary"),
        name="s5_scan_bwd" if reverse else "s5_scan_fwd",
    )(*args)


HY_CT = 512
HY_FEAT_PAD = V7X_LANES
HY_DECAY_COLS = HY_ORDER * 2 * HY_W


def _short_conv_kernel(prev_ref, u_ref, next_ref, w_ref, b_ref, v_ref, x1_ref, x2_ref, *, n_ctx, n_total):
    i = pl.program_id(1)
    u = u_ref[...].astype(_F32)
    rows = u.shape[0]
    r = lax.broadcasted_iota(jnp.int32, u.shape, 0)
    g = r + i * rows
    before = jnp.broadcast_to(prev_ref[V7X_SUBLANES - 1:V7X_SUBLANES, :].astype(_F32), u.shape)
    after = jnp.broadcast_to(next_ref[0:1, :].astype(_F32), u.shape)
    up = jnp.where(r == 0, before, pltpu.roll(u, 1, 0))
    un = jnp.where(r == rows - 1, after, pltpu.roll(u, rows - 1, 0))
    up = jnp.where((g == 0) | (g == n_ctx), 0.0, up)
    un = jnp.where((g == n_ctx - 1) | (g == n_total - 1), 0.0, un)
    us = w_ref[0:1, :] * up + w_ref[1:2, :] * u + w_ref[2:3, :] * un + b_ref[...]
    v_ref[...] = us[:, :HY_W].astype(v_ref.dtype)
    x1_ref[...] = us[:, HY_W:2 * HY_W].astype(x1_ref.dtype)
    x2_ref[...] = us[:, 2 * HY_W:].astype(x2_ref.dtype)


def hyena_short_conv(hy, conv_w, conv_b, layer, n_ctx):
    bsz, t, width = hy.shape
    per = ROW_TILE // V7X_SUBLANES
    n_halo = t // V7X_SUBLANES
    row = lambda w: pl.BlockSpec((None, ROW_TILE, w), lambda b, i: (b, i, 0))
    return pl.pallas_call(
        functools.partial(_short_conv_kernel, n_ctx=n_ctx, n_total=t),
        grid=(bsz, t // ROW_TILE),
        in_specs=[
            pl.BlockSpec((None, V7X_SUBLANES, width), lambda b, i: (b, jnp.maximum(i * per - 1, 0), 0)),
            row(width),
            pl.BlockSpec((None, V7X_SUBLANES, width), lambda b, i: (b, jnp.minimum((i + 1) * per, n_halo - 1), 0)),
            pl.BlockSpec((None, conv_w.shape[1], width), lambda b, i: (layer, 0, 0)),
            pl.BlockSpec((None, 1, width), lambda b, i: (layer, 0, 0)),
        ],
        out_specs=[row(HY_W)] * 3,
        out_shape=[jax.ShapeDtypeStruct((bsz, t, HY_W), _BF16)] * 3,
        compiler_params=_cparams("parallel", "parallel"),
        name="hyena_short_conv",
    )(hy, hy, hy, conv_w, conv_b)


def hyena_features(n):
    t = jnp.arange(n, dtype=_F32)
    t_norm = t / (n - 1)
    bands = jnp.linspace(1e-4, HY_BANDS - 1, HY_BANDS, dtype=_F32)
    ang = (2.0 * math.pi * t / n)[:, None] * bands[None, :]
    feat = jnp.concatenate([t_norm[:, None], jnp.cos(ang), -jnp.sin(ang)], -1)
    feat = jnp.pad(feat, ((0, 0), (0, HY_FEAT_PAD - HY_EMB)))
    return feat, jnp.broadcast_to(t_norm[:, None], (n, V7X_LANES))


def _filter_kernel(feat_ref, tn_ref, w1_ref, b1_ref, w2_ref, b2_ref, w3_ref, fr_ref, dec_ref, h_ref, l1_ref):
    i = pl.program_id(0)
    hp = lax.Precision.HIGHEST
    fr = fr_ref[...]
    h = jnp.sin(fr * (jnp.dot(feat_ref[...], w1_ref[...], preferred_element_type=_F32, precision=hp) + b1_ref[...]))
    h = jnp.sin(fr * (jnp.dot(h, w2_ref[...], preferred_element_type=_F32, precision=hp) + b2_ref[...]))
    h = _dot(h.astype(_BF16), w3_ref[...])
    tn = tn_ref[...]
    rows = h.shape[0]
    first = (lax.broadcasted_iota(jnp.int32, (rows, V7X_LANES), 0) + i * rows) == 0

    @pl.when(i == 0)
    def _():
        l1_ref[...] = jnp.zeros_like(l1_ref)

    for c in range(HY_DECAY_COLS // V7X_LANES):
        sl = slice(c * V7X_LANES, (c + 1) * V7X_LANES)
        blk = h[:, sl] * jnp.exp(-tn * jnp.abs(dec_ref[:, sl]))
        anti = (c * V7X_LANES // HY_W) % 2 == 1
        if anti:
            blk = jnp.where(first, 0.0, blk)
        h_ref[:, sl] = blk.astype(h_ref.dtype)
        l1_ref[:, sl] += jnp.sum(jnp.abs(blk).reshape(rows // V7X_SUBLANES, V7X_SUBLANES, V7X_LANES), 0)


def hyena_filter_taps(n, w1, b1, w2, b2, w3, freq, decay, layer):
    feat, tn = hyena_features(n)
    rows = min(ROW_TILE, n)
    lay = lambda *shape: pl.BlockSpec((None,) + shape, lambda i: (layer,) + (0,) * len(shape))
    return pl.pallas_call(
        _filter_kernel,
        grid=(n // rows,),
        in_specs=[
            pl.BlockSpec((rows, HY_FEAT_PAD), lambda i: (i, 0)),
            pl.BlockSpec((rows, V7X_LANES), lambda i: (i, 0)),
            lay(HY_FEAT_PAD, HY_HIDDEN), lay(1, HY_HIDDEN), lay(HY_HIDDEN, HY_HIDDEN), lay(1, HY_HIDDEN),
            lay(HY_HIDDEN, HY_DECAY_COLS), lay(1, HY_HIDDEN), lay(1, HY_DECAY_COLS),
        ],
        out_specs=[pl.BlockSpec((rows, HY_DECAY_COLS), lambda i: (i, 0)),
                   pl.BlockSpec((V7X_SUBLANES, HY_DECAY_COLS), lambda i: (0, 0))],
        out_shape=[jax.ShapeDtypeStruct((n, HY_DECAY_COLS), _BF16),
                   jax.ShapeDtypeStruct((V7X_SUBLANES, HY_DECAY_COLS), _F32)],
        compiler_params=_cparams("arbitrary"),
        name="hyena_filter_taps",
    )(feat, tn, w1, b1, w2, b2, w3, freq, decay)


def _dft_block(n_out, n_in, sign, scale=1.0, real_input=False):
    size = max(n_out, n_in)
    ang = 2.0 * np.pi * np.outer(np.arange(n_out), np.arange(n_in)) / size
    fr, fi = np.cos(ang) * scale, sign * np.sin(ang) * scale
    blk = np.concatenate([fr, fi], 0) if real_input else np.block([[fr, -fi], [fi, fr]])
    return jnp.asarray(blk, _F32).astype(_BF16)


def _twiddle_table(n1, n2, sign):
    ang = 2.0 * np.pi * np.outer(np.arange(n1), np.arange(n2)) / (n1 * n2)
    tw = np.stack([np.cos(ang), sign * np.sin(ang)], 0)[..., None]
    return jnp.broadcast_to(jnp.asarray(tw, _F32), (2, n1, n2, V7X_LANES))


def _cmul(ar, ai, br, bi):
    return ar * br - ai * bi, ar * bi + ai * br


def _lanes_mul(xr, xi, twr, twi):
    outs_r, outs_i = [], []
    for c in range(xr.shape[1] // V7X_LANES):
        sl = slice(c * V7X_LANES, (c + 1) * V7X_LANES)
        r, i = _cmul(xr[:, sl], xi[:, sl], twr, twi)
        outs_r.append(r)
        outs_i.append(i)
    return jnp.concatenate(outs_r, 1), jnp.concatenate(outs_i, 1)


def _dft_kernel(*refs, real_input, pre_tw, has_filter, has_second, post_tw, slabs):
    refs = list(refs)
    xr_ref = refs.pop(0)
    xi_ref = None if real_input else refs.pop(0)
    f_ref = refs.pop(0)
    tw_ref = refs.pop(0) if (pre_tw or post_tw) else None
    if has_filter:
        hr_ref, hi_ref, l1c_ref, l1a_ref = refs.pop(0), refs.pop(0), refs.pop(0), refs.pop(0)
    g_ref = refs.pop(0) if has_second else None
    or_ref, oi_ref = refs
    for a in range(slabs):
        xr = xr_ref[a].astype(_F32)
        if real_input:
            s = xr.astype(_BF16)
        else:
            xi = xi_ref[a].astype(_F32)
            if pre_tw:
                xr, xi = _lanes_mul(xr, xi, tw_ref[0, a], tw_ref[1, a])
            s = jnp.concatenate([xr, xi], 0).astype(_BF16)
        y = _dot(f_ref[...], s)
        m = y.shape[0] // 2
        yr, yi = y[:m], y[m:]
        if has_filter:
            inv = 1.0 / (jnp.sum(l1c_ref[...] + l1a_ref[...], 0, keepdims=True) + EPS)
            yr, yi = _cmul(yr, yi, hr_ref[a].astype(_F32) * inv, hi_ref[a].astype(_F32) * inv)
        if has_second:
            z = _dot(g_ref[...], jnp.concatenate([yr, yi], 0).astype(_BF16))
            m = z.shape[0] // 2
            yr, yi = z[:m], z[m:]
        if post_tw:
            yr, yi = _lanes_mul(yr, yi, tw_ref[0, a], -tw_ref[1, a])
        or_ref[a] = yr.astype(or_ref.dtype)
        oi_ref[a] = yi.astype(oi_ref.dtype)


def dft_apply(xr, xi, f_blk, *, tw=None, pre_tw=False, post_tw=False, filt=None, g_blk=None, out_dtype=_BF16):
    n_a, k, c = xr.shape
    real_input = xi is None
    m1 = f_blk.shape[0] // 2
    m_out = g_blk.shape[0] // 2 if g_blk is not None else m1
    ct = min(HY_CT, c)
    slabs = max(1, min(n_a, 1024 // max(k, m_out)))
    slab_spec = lambda rows: pl.BlockSpec((slabs, rows, ct), lambda j, a: (a, 0, j))
    whole = lambda arr: pl.BlockSpec(arr.shape, lambda j, a: (0,) * arr.ndim)
    in_specs, args = [slab_spec(k)], [xr]
    if not real_input:
        in_specs.append(slab_spec(k))
        args.append(xi)
    in_specs.append(whole(f_blk))
    args.append(f_blk)
    if pre_tw or post_tw:
        in_specs.append(pl.BlockSpec((2, slabs, tw.shape[2], V7X_LANES), lambda j, a: (0, a, 0, 0)))
        args.append(tw)
    if filt is not None:
        l1_spec = pl.BlockSpec((V7X_SUBLANES, ct), lambda j, a: (0, j))
        in_specs += [slab_spec(m1), slab_spec(m1), l1_spec, l1_spec]
        args += list(filt)
    if g_blk is not None:
        in_specs.append(whole(g_blk))
        args.append(g_blk)
    kern = functools.partial(_dft_kernel, real_input=real_input, pre_tw=pre_tw, has_filter=filt is not None,
                             has_second=g_blk is not None, post_tw=post_tw, slabs=slabs)
    return pl.pallas_call(
        kern,
        grid=(c // ct, n_a // slabs),
        in_specs=in_specs,
        out_specs=[slab_spec(m_out)] * 2,
        out_shape=[jax.ShapeDtypeStruct((n_a, m_out, c), out_dtype)] * 2,
        compiler_params=_cparams("parallel", "parallel"),
        name="hyena_dft",
    )(*args)


def _fft_factors(n_fft):
    n1 = 1 << ((n_fft.bit_length() - 1) // 2)
    return n_fft // n1, n1


def filter_spectrum(taps):
    n_fft, c = taps.shape
    if n_fft <= 512:
        return dft_apply(taps[None], None, _dft_block(n_fft, n_fft, -1.0, real_input=True))
    n1, n2 = _fft_factors(n_fft)
    x = jnp.swapaxes(taps.reshape(n1, n2, c), 0, 1)
    yr, yi = dft_apply(x, None, _dft_block(n1, n1, -1.0, real_input=True))
    yr, yi = jnp.swapaxes(yr, 0, 1), jnp.swapaxes(yi, 0, 1)
    return dft_apply(yr, yi, _dft_block(n2, n2, -1.0), tw=_twiddle_table(n1, n2, -1.0), pre_tw=True)


def long_conv_pair(vr, vi, filt):
    n, c = vr.shape
    n_fft = 2 * n
    if n_fft <= 512:
        f = _dft_block(n_fft, n, -1.0)
        g = _dft_block(n, n_fft, 1.0, scale=1.0 / n_fft)
        yr, yi = dft_apply(vr[None], vi[None], f, filt=filt, g_blk=g)
        return yr[0], yi[0]
    n1, n2 = _fft_factors(n_fft)
    n1h = n1 // 2
    to_slabs = lambda a: jnp.swapaxes(a.reshape(n1h, n2, c), 0, 1)
    yr, yi = dft_apply(to_slabs(vr), to_slabs(vi), _dft_block(n1, n1h, -1.0))
    yr, yi = jnp.swapaxes(yr, 0, 1), jnp.swapaxes(yi, 0, 1)
    tw = _twiddle_table(n1, n2, -1.0)
    zr, zi = dft_apply(yr, yi, _dft_block(n2, n2, -1.0), tw=tw, pre_tw=True, post_tw=True, filt=filt,
                       g_blk=_dft_block(n2, n2, 1.0))
    zr, zi = jnp.swapaxes(zr, 0, 1), jnp.swapaxes(zi, 0, 1)
    outr, outi = dft_apply(zr, zi, _dft_block(n1h, n1, 1.0, scale=1.0 / n_fft))
    back = lambda a: jnp.swapaxes(a, 0, 1).reshape(n, c)
    return back(outr), back(outi)


def _gate_kernel(x_ref, conv_ref, v_ref, skip_ref, o_ref):
    v = v_ref[...].astype(_F32)
    o_ref[...] = (x_ref[...].astype(_F32) * (conv_ref[...].astype(_F32) + skip_ref[...] * v)).astype(o_ref.dtype)


def hyena_gate(xg, conv, v, skip, layer, order):
    bsz, n, _ = xg.shape
    rows = min(ROW_TILE, n)
    row = pl.BlockSpec((None, rows, HY_W), lambda b, i: (b, i, 0))
    return pl.pallas_call(
        _gate_kernel,
        grid=(bsz, n // rows),
        in_specs=[row, row, row, pl.BlockSpec((None, None, 1, HY_W), lambda b, i: (layer, order, 0, 0))],
        out_specs=row,
        out_shape=jax.ShapeDtypeStruct((bsz, n, HY_W), _BF16),
        compiler_params=_cparams("parallel", "parallel"),
        name="hyena_gate",
    )(xg, conv, v, skip)


def hyena_segment(v, x1, x2, w1, b1, w2, b2, w3, freq, decay, skip, layer):
    bsz, n, _ = v.shape
    assert bsz == 2, "the two batch elements are packed into one complex transform"
    h, l1 = hyena_filter_taps(n, w1, b1, w2, b2, w3, freq, decay, layer)
    h = h.reshape(n, HY_ORDER, 2, HY_W)
    taps = jnp.concatenate([h[:, :, 0], jnp.zeros((1, HY_ORDER, HY_W), h.dtype), jnp.flip(h[1:, :, 1], 0)], 0)
    l1 = l1.reshape(V7X_SUBLANES, HY_ORDER, 2, HY_W)
    hr, hi = filter_spectrum(taps.reshape(2 * n, HY_ORDER * HY_W))
    u = v
    for o, xg in enumerate((x1, x2)):
        sl = slice(o * HY_W, (o + 1) * HY_W)
        filt = (hr[..., sl], hi[..., sl], l1[:, o, 0], l1[:, o, 1])
        c0, c1 = long_conv_pair(u[0], u[1], filt)
        u = hyena_gate(xg, jnp.stack([c0, c1], 0), u, skip, layer, o)
    return u


def pack_hyena_weights(p):
    depth = p["hy_w1"].shape[0]
    w1 = jnp.pad(p["hy_w1"], ((0, 0), (0, HY_FEAT_PAD - HY_EMB), (0, 0)))
    row = lambda a: a.reshape(depth, 1, -1)
    return (w1, row(p["hy_b1"]), p["hy_w2"], row(p["hy_b2"]), p["hy_w3"].astype(_BF16), row(p["hy_freq"]),
            row(p["hy_decay"]), p["hy_skip"].reshape(depth, HY_ORDER, 1, HY_W))


def kernel(x, c, ctx, c_ctx, w_mod, b_mod, norm1, norm2, w_in, ssm_lam_re, ssm_lam_im, ssm_log_dt, ssm_b_re, ssm_b_im, ssm_c_re, ssm_c_im, ssm_d, ssm_w_glu, ssm_b_glu, mla_q_norm, mla_w_uq, mla_kv_norm, mla_w_ukv, qk_norm_q, qk_norm_k, hy_conv_w, hy_conv_b, hy_w1, hy_b1, hy_w2, hy_b2, hy_w3, hy_freq, hy_decay, hy_skip, w_branch, w_out, router_w, router_bias, moe_w_gate, moe_w_up, moe_w_down):
    depth = w_mod.shape[0]
    bsz, n_lat, _ = x.shape
    n_ctx = ctx.shape[1]
    assert n_ctx % ROW_TILE == 0 and n_lat % ROW_TILE == 0 and ROW_TILE == ATT_TILE
    n_ctx_tiles = n_ctx // ROW_TILE
    row = lambda a: a.reshape(depth, 1, -1)

    mod_tab = modulation_table(c, c_ctx, w_mod, b_mod)
    w_in_used = pack_w_in(w_in)
    w_gates = w_in[:, :, _IN_GATES:].astype(_BF16)
    w_branch_b, w_out_b = w_branch.astype(_BF16), w_out.astype(_BF16)
    moe_b = (moe_w_gate.astype(_BF16), moe_w_up.astype(_BF16), moe_w_down.astype(_BF16))
    rw_hi, rw_lo, rb_col = split_router(router_w, router_bias)
    norm1_r, norm2_r = row(norm1), row(norm2)
    s5_consts, s5_bd, s5_cd = pack_s5(*s5_discretize(ssm_lam_re, ssm_lam_im, ssm_log_dt, ssm_b_re, ssm_b_im),
                                      ssm_c_re, ssm_c_im)
    s5_finish = (row(ssm_d), ssm_w_glu.astype(_BF16), row(ssm_b_glu))
    mla_packed = pack_mla_weights(mla_w_uq, mla_w_ukv, qk_norm_q, qk_norm_k)
    mla_tables = rope_tables(n_ctx, n_lat)
    hy_w = pack_hyena_weights(dict(hy_w1=hy_w1, hy_b1=hy_b1, hy_w2=hy_w2, hy_b2=hy_b2, hy_w3=hy_w3, hy_freq=hy_freq,
                                   hy_decay=hy_decay, hy_skip=hy_skip))
    hy_conv_b_r = row(hy_conv_b)

    xcat = jnp.concatenate([ctx, x], 1)
    for l in range(depth):
        u, q_lat, kv_lat, hy, k_rope = input_projection(xcat, mod_tab, norm1_r, w_in_used, l, n_ctx_tiles)

        y_fwd = s5_scan(u, s5_consts, s5_bd, s5_cd, l, n_ctx_tiles, False)
        y_ssm = s5_scan(u, s5_consts, s5_bd, s5_cd, l, n_ctx_tiles, True, (y_fwd,) + s5_finish)

        q_t, k, v_t = mla_prepare(q_lat, kv_lat, k_rope, row(mla_q_norm), row(mla_kv_norm), mla_packed, mla_tables, l)
        y_att = jnp.swapaxes(flash_attention(q_t, k, v_t, n_ctx_tiles), 1, 2)

        v, x1, x2 = hyena_short_conv(hy, hy_conv_w, hy_conv_b_r, l, n_ctx)
        lat = lambda a: a[:, n_ctx:]
        hy_lat = hyena_segment(lat(v), lat(x1), lat(x2), *hy_w, l)
        if l < depth - 1:
            head = lambda a: a[:, :n_ctx]
            hy_ctx = hyena_segment(head(v), head(x1), head(x2), *hy_w, l)
        else:
            hy_ctx = jnp.zeros((bsz, n_ctx, HY_W), hy_lat.dtype)
        y_hy = jnp.concatenate([hy_ctx, hy_lat], 1)

        x_new, hl, route = merge_and_route(xcat, y_ssm, y_att, y_hy, mod_tab, norm1_r, norm2_r, w_gates, w_branch_b,
                                           w_out_b, rw_hi, rw_lo, rb_col, l, n_ctx_tiles)
        xcat = moe_block(x_new, hl, route, mod_tab, *moe_b, l, n_ctx_tiles)
    return xcat[:, n_ctx:]
```

```python
import functools
import math

import numpy as np
import jax
import jax.numpy as jnp
from jax import lax
from jax.experimental import pallas as pl
from jax.experimental.pallas import tpu as pltpu

D_MODEL = 1024
EPS = 1e-6
GRID_W = 64
SSM_W = 512
SSM_GROUP = 16
SSM_GROUPS = SSM_W // SSM_GROUP
SSM_STATE = 64
MLA_HEADS = 8
MLA_Q_RANK = 384
MLA_KV_RANK = 256
MLA_NOPE = 64
MLA_ROPE = 32
MLA_V = 64
MLA_QK = MLA_NOPE + MLA_ROPE
ROPE_THETA = 10000.0
HY_W = 512
HY_ORDER = 2
HY_BANDS = 16
HY_EMB = 1 + 2 * HY_BANDS
HY_HIDDEN = 64
N_BRANCH = 3
IN_SPLITS = (SSM_W, MLA_Q_RANK, MLA_KV_RANK, MLA_ROPE, 3 * HY_W)
N_EXPERTS = 16
N_EXPERT_GROUPS = 4
EXPERTS_PER_GROUP = N_EXPERTS // N_EXPERT_GROUPS
D_EXPERT = 512

V7X_LANES = 128
V7X_SUBLANES = 8
V7X_VMEM_LIMIT = 56 * 1024 * 1024

ROW_TILE = 256
MOE_TILE = 256
HEAD_PAD = 128

_F32 = jnp.float32
_BF16 = jnp.bfloat16


def _cparams(*sem):
    return pltpu.CompilerParams(dimension_semantics=sem, vmem_limit_bytes=V7X_VMEM_LIMIT)


def _silu(v):
    return v * jax.nn.sigmoid(v)


def _modulated_norm(x, g, shift, scale):
    y = x * lax.rsqrt(jnp.mean(x * x, -1, keepdims=True) + EPS)
    return (y * g) * (1.0 + scale) + shift


def _dot(a, b):
    return jnp.dot(a, b, preferred_element_type=_F32)


def _mod_kernel(c_ref, w_ref, b_ref, o_ref):
    s = _silu(c_ref[...])
    o_ref[...] = jnp.dot(s, w_ref[...], preferred_element_type=_F32, precision=lax.Precision.HIGHEST) + b_ref[...]


def modulation_table(c, c_ctx, w_mod, b_mod):
    depth = w_mod.shape[0]
    bsz = c.shape[0]
    rows = V7X_SUBLANES * pl.cdiv(bsz + 1, V7X_SUBLANES)
    cvec = jnp.concatenate([c, c_ctx[None, :], jnp.zeros((rows - bsz - 1, D_MODEL), _F32)], 0)
    out = pl.pallas_call(
        _mod_kernel,
        grid=(depth, 6),
        in_specs=[
            pl.BlockSpec((rows, D_MODEL), lambda l, j: (0, 0)),
            pl.BlockSpec((None, D_MODEL, D_MODEL), lambda l, j: (l, 0, j)),
            pl.BlockSpec((None, 1, D_MODEL), lambda l, j: (l, 0, j)),
        ],
        out_specs=pl.BlockSpec((None, rows, D_MODEL), lambda l, j: (l, 0, j)),
        out_shape=jax.ShapeDtypeStruct((depth, rows, 6 * D_MODEL), _F32),
        compiler_params=_cparams("parallel", "parallel"),
        name="modulation",
    )(cvec, w_mod, b_mod.reshape(depth, 1, 6 * D_MODEL))
    m = out.reshape(depth, rows, 6, D_MODEL)
    lat = m[:, :bsz]
    ctx = jnp.broadcast_to(m[:, bsz:bsz + 1], lat.shape)
    tab = jnp.stack([ctx, lat], 2)
    return jnp.pad(tab, ((0, 0), (0, 0), (0, 0), (0, V7X_SUBLANES - 6), (0, 0)))


def _mod_spec(layer, n_ctx_tiles):
    return pl.BlockSpec((None, None, None, V7X_SUBLANES, D_MODEL),
                        lambda b, i: (layer, b, jnp.where(i >= n_ctx_tiles, 1, 0), 0, 0))


_IN_OFF = (0, SSM_W, SSM_W + MLA_Q_RANK, SSM_W + MLA_Q_RANK + MLA_KV_RANK)
_IN_HY = _IN_OFF[3]
_IN_ROPE = _IN_HY + 3 * HY_W
_IN_USED = _IN_ROPE + MLA_ROPE
_IN_GATES = sum(IN_SPLITS)


def pack_w_in(w_in):
    a = SSM_W + MLA_Q_RANK + MLA_KV_RANK
    return jnp.concatenate([w_in[:, :, :a], w_in[:, :, a + MLA_ROPE:_IN_GATES], w_in[:, :, a:a + MLA_ROPE]],
                           -1).astype(_BF16)


def split_router(router_w, router_bias):
    rw = router_w.T
    hi = rw.astype(_BF16)
    lo = (rw - hi.astype(_F32)).astype(_BF16)
    return hi, lo, router_bias.reshape(N_EXPERTS, 1).astype(_F32)


def _inproj_kernel(x_ref, mod_ref, g_ref, w_ref, u_ref, q_ref, kv_ref, hy_ref, kr_ref):
    xn = _modulated_norm(x_ref[...], g_ref[...], mod_ref[0:1, :], mod_ref[1:2, :]).astype(_BF16)
    u_ref[...] = _dot(xn, w_ref[:, _IN_OFF[0]:_IN_OFF[1]]).astype(u_ref.dtype)
    q_ref[...] = _dot(xn, w_ref[:, _IN_OFF[1]:_IN_OFF[2]]).astype(q_ref.dtype)
    kv_ref[...] = _dot(xn, w_ref[:, _IN_OFF[2]:_IN_OFF[3]]).astype(kv_ref.dtype)
    hy_ref[...] = _dot(xn, w_ref[:, _IN_HY:_IN_ROPE]).astype(hy_ref.dtype)
    kr_ref[...] = _dot(xn, w_ref[:, _IN_ROPE:_IN_USED]).astype(kr_ref.dtype)


def input_projection(x, mod_tab, norm1, w_in_used, layer, n_ctx_tiles):
    bsz, t, _ = x.shape
    widths = (SSM_W, MLA_Q_RANK, MLA_KV_RANK, 3 * HY_W, MLA_ROPE)
    row = lambda w: pl.BlockSpec((None, ROW_TILE, w), lambda b, i: (b, i, 0))
    return pl.pallas_call(
        _inproj_kernel,
        grid=(bsz, t // ROW_TILE),
        in_specs=[
            row(D_MODEL),
            _mod_spec(layer, n_ctx_tiles),
            pl.BlockSpec((None, 1, D_MODEL), lambda b, i: (layer, 0, 0)),
            pl.BlockSpec((None, D_MODEL, _IN_USED), lambda b, i: (layer, 0, 0)),
        ],
        out_specs=[row(w) for w in widths],
        out_shape=[jax.ShapeDtypeStruct((bsz, t, w), _BF16) for w in widths],
        compiler_params=_cparams("parallel", "parallel"),
        name="input_projection",
    )(x, mod_tab, norm1, w_in_used)


def _max4(a, b, c, d):
    hi1, lo1 = jnp.maximum(a, b), jnp.minimum(a, b)
    hi2, lo2 = jnp.maximum(c, d), jnp.minimum(c, d)
    return jnp.maximum(hi1, hi2), jnp.maximum(jnp.minimum(hi1, hi2), jnp.maximum(lo1, lo2))


def _first_argmax(vals):
    best, idx = vals[0], jnp.zeros(vals[0].shape, jnp.int32)
    for j in range(1, len(vals)):
        upd = vals[j] > best
        idx = jnp.where(upd, j, idx)
        best = jnp.where(upd, vals[j], best)
    return idx


def _pick(vals, idx):
    out = vals[-1]
    for j in range(len(vals) - 2, -1, -1):
        out = jnp.where(idx == j, vals[j], out)
    return out


def _route_rows(logits_t, bias_col):
    s = jax.nn.sigmoid(logits_t)
    sel = s + bias_col
    s_rows = [s[e:e + 1, :] for e in range(N_EXPERTS)]
    sel_rows = [sel[e:e + 1, :] for e in range(N_EXPERTS)]
    gsum = []
    for g in range(N_EXPERT_GROUPS):
        top, second = _max4(*sel_rows[EXPERTS_PER_GROUP * g:EXPERTS_PER_GROUP * (g + 1)])
        gsum.append(top + second)
    gidx = _first_argmax(gsum)
    v = [_pick([sel_rows[EXPERTS_PER_GROUP * g + j] for g in range(N_EXPERT_GROUPS)], gidx)
         for j in range(EXPERTS_PER_GROUP)]
    u = [_pick([s_rows[EXPERTS_PER_GROUP * g + j] for g in range(N_EXPERT_GROUPS)], gidx)
         for j in range(EXPERTS_PER_GROUP)]
    i1 = _first_argmax(v)
    neg = jnp.full(v[0].shape, -jnp.inf, _F32)
    i2 = _first_argmax([jnp.where(i1 == j, neg, v[j]) for j in range(EXPERTS_PER_GROUP)])
    wa, wb = _pick(u, i1), _pick(u, i2)
    tot = wa + wb
    e0 = (EXPERTS_PER_GROUP * gidx + i1).astype(_F32)
    e1 = (EXPERTS_PER_GROUP * gidx + i2).astype(_F32)
    return e0, e1, wa / tot, wb / tot


def _merge_kernel(x_ref, ys_ref, ya_ref, yhc_ref, yhl_ref, mod_ref, g1_ref, g2_ref, wg_ref, wb_ref, wo_ref,
                  rwh_ref, rwl_ref, rb_ref, xo_ref, hl_ref, rt_ref, *, n_ctx_tiles):
    x = x_ref[...]
    xn = _modulated_norm(x, g1_ref[...], mod_ref[0:1, :], mod_ref[1:2, :]).astype(_BF16)
    y_hy = jnp.where(pl.program_id(1) < n_ctx_tiles, yhc_ref[...], yhl_ref[...])
    m = None
    for k, y in enumerate((ys_ref[...], ya_ref[...], y_hy)):
        gate = jax.nn.sigmoid(_dot(xn, wg_ref[:, k * D_MODEL:(k + 1) * D_MODEL]))
        term = gate * _dot(y, wb_ref[k])
        m = term if m is None else m + term
    x_new = x + mod_ref[2:3, :] * _dot(m.astype(_BF16), wo_ref[...])
    xo_ref[...] = x_new
    hl = _modulated_norm(x_new, g2_ref[...], mod_ref[3:4, :], mod_ref[4:5, :])
    hl_ref[...] = hl
    hl_hi = hl.astype(_BF16)
    hl_lo = (hl - hl_hi.astype(_F32)).astype(_BF16)
    nt = (((1,), (1,)), ((), ()))
    logits_t = (lax.dot_general(rwh_ref[...], hl_hi, nt, preferred_element_type=_F32)
                + lax.dot_general(rwl_ref[...], hl_hi, nt, preferred_element_type=_F32)
                + lax.dot_general(rwh_ref[...], hl_lo, nt, preferred_element_type=_F32))
    e0, e1, w0, w1 = _route_rows(logits_t, rb_ref[...])
    zero = jnp.zeros_like(w0)
    rt_ref[...] = jnp.concatenate([e0, e1, w0, w1, zero, zero, zero, zero], 0)


def merge_and_route(x, y_ssm, y_att, hy_ctx, hy_lat, mod_tab, norm1, norm2, w_gates, w_branch, w_out,
                    rw_hi, rw_lo, rb_col, layer, n_ctx_tiles):
    bsz, t, _ = x.shape
    row = lambda w: pl.BlockSpec((None, ROW_TILE, w), lambda b, i: (b, i, 0))
    whole = lambda *shape: pl.BlockSpec(shape, lambda b, i: (0,) * len(shape))
    return pl.pallas_call(
        functools.partial(_merge_kernel, n_ctx_tiles=n_ctx_tiles),
        grid=(bsz, t // ROW_TILE),
        in_specs=[
            row(D_MODEL), row(SSM_W), row(MLA_HEADS * MLA_V),
            pl.BlockSpec((None, ROW_TILE, HY_W), lambda b, i: (b, jnp.minimum(i, n_ctx_tiles - 1), 0)),
            pl.BlockSpec((None, ROW_TILE, HY_W), lambda b, i: (b, jnp.maximum(i - n_ctx_tiles, 0), 0)),
            _mod_spec(layer, n_ctx_tiles),
            pl.BlockSpec((None, 1, D_MODEL), lambda b, i: (layer, 0, 0)),
            pl.BlockSpec((None, 1, D_MODEL), lambda b, i: (layer, 0, 0)),
            pl.BlockSpec((None, D_MODEL, N_BRANCH * D_MODEL), lambda b, i: (layer, 0, 0)),
            pl.BlockSpec((None, N_BRANCH, SSM_W, D_MODEL), lambda b, i: (layer, 0, 0, 0)),
            pl.BlockSpec((None, D_MODEL, D_MODEL), lambda b, i: (layer, 0, 0)),
            whole(N_EXPERTS, D_MODEL), whole(N_EXPERTS, D_MODEL), whole(N_EXPERTS, 1),
        ],
        out_specs=[row(D_MODEL), row(D_MODEL),
                   pl.BlockSpec((None, V7X_SUBLANES, ROW_TILE), lambda b, i: (b, 0, i))],
        out_shape=[jax.ShapeDtypeStruct((bsz, t, D_MODEL), _F32),
                   jax.ShapeDtypeStruct((bsz, t, D_MODEL), _F32),
                   jax.ShapeDtypeStruct((bsz, V7X_SUBLANES, t), _F32)],
        compiler_params=_cparams("parallel", "parallel"),
        name="merge_and_route",
    )(x, y_ssm, y_att, hy_ctx, hy_lat, mod_tab, norm1, norm2, w_gates, w_branch, w_out, rw_hi, rw_lo, rb_col)


def _moe_kernel(te_ref, tv_ref, xs_ref, wg_ref, wu_ref, wd_ref, o_ref):
    del te_ref

    @pl.when(tv_ref[pl.program_id(0)] > 0)
    def _():
        h = xs_ref[...].astype(_BF16)
        act = (_silu(_dot(h, wg_ref[...])) * _dot(h, wu_ref[...])).astype(_BF16)
        o_ref[...] = _dot(act, wd_ref[...]).astype(o_ref.dtype)

    @pl.when(tv_ref[pl.program_id(0)] == 0)
    def _():
        o_ref[...] = jnp.zeros_like(o_ref)


def grouped_experts(xs, tile_expert, tile_valid, w_gate, w_up, w_down, layer):
    rows = xs.shape[0]
    n_tiles = rows // MOE_TILE
    wspec = lambda a, b: pl.BlockSpec((None, None, a, b), lambda i, te, tv: (layer, te[i], 0, 0))
    return pl.pallas_call(
        _moe_kernel,
        grid_spec=pltpu.PrefetchScalarGridSpec(
            num_scalar_prefetch=2,
            grid=(n_tiles,),
            in_specs=[
                pl.BlockSpec((MOE_TILE, D_MODEL), lambda i, te, tv: (i, 0)),
                wspec(D_MODEL, D_EXPERT), wspec(D_MODEL, D_EXPERT), wspec(D_EXPERT, D_MODEL),
            ],
            out_specs=pl.BlockSpec((MOE_TILE, D_MODEL), lambda i, te, tv: (i, 0)),
        ),
        out_shape=jax.ShapeDtypeStruct((rows, D_MODEL), _BF16),
        compiler_params=_cparams("arbitrary"),
        name="grouped_experts",
    )(tile_expert, tile_valid, xs, w_gate, w_up, w_down)


def expert_dispatch_plan(route):
    bsz, _, t = route.shape
    n = bsz * t
    e = jnp.stack([route[:, 0, :], route[:, 1, :]], -1).reshape(n * 2).astype(jnp.int32)
    onehot = (e[:, None] == jnp.arange(N_EXPERTS, dtype=jnp.int32)[None, :]).astype(jnp.int32)
    rank = jnp.take_along_axis(jnp.cumsum(onehot, 0) - onehot, e[:, None], 1)[:, 0]
    count = jnp.sum(onehot, 0)
    tiles = (count + MOE_TILE - 1) // MOE_TILE
    tile_start = jnp.cumsum(tiles) - tiles
    dest = tile_start[e] * MOE_TILE + rank
    rows = 2 * n + N_EXPERTS * MOE_TILE
    n_tiles = rows // MOE_TILE
    tile_ids = jnp.arange(n_tiles, dtype=jnp.int32)
    tile_expert = jnp.clip(jnp.searchsorted(jnp.cumsum(tiles), tile_ids, side="right"), 0, N_EXPERTS - 1)
    tile_valid = (tile_ids < jnp.sum(tiles)).astype(jnp.int32)
    src = jnp.zeros((rows,), jnp.int32).at[dest].set(jnp.arange(2 * n, dtype=jnp.int32) // 2)
    dest2 = dest.reshape(n, 2)
    return src, tile_expert.astype(jnp.int32), tile_valid, dest2[:, 0], dest2[:, 1]


def _combine_kernel(x_ref, a_ref, b_ref, wa_ref, wb_ref, mod_ref, o_ref):
    wa, wb = wa_ref[...], wb_ref[...]
    for j in range(D_MODEL // V7X_LANES):
        sl = slice(j * V7X_LANES, (j + 1) * V7X_LANES)
        y = wa * a_ref[:, sl].astype(_F32) + wb * b_ref[:, sl].astype(_F32)
        o_ref[:, sl] = x_ref[:, sl] + mod_ref[5:6, sl] * y


def combine_experts(x, ya, yb, wa, wb, mod_tab, layer, n_ctx_tiles):
    bsz, t, _ = x.shape
    row = lambda w: pl.BlockSpec((None, ROW_TILE, w), lambda b, i: (b, i, 0))
    return pl.pallas_call(
        _combine_kernel,
        grid=(bsz, t // ROW_TILE),
        in_specs=[row(D_MODEL), row(D_MODEL), row(D_MODEL), row(V7X_LANES), row(V7X_LANES),
                  _mod_spec(layer, n_ctx_tiles)],
        out_specs=row(D_MODEL),
        out_shape=jax.ShapeDtypeStruct(x.shape, _F32),
        compiler_params=_cparams("parallel", "parallel"),
        name="combine_experts",
    )(x, ya, yb, wa, wb, mod_tab)


def moe_block(x_new, hl, route, mod_tab, w_gate, w_up, w_down, layer, n_ctx_tiles):
    bsz, t, _ = x_new.shape
    src, tile_expert, tile_valid, d0, d1 = expert_dispatch_plan(route)
    xs = jnp.take(hl.reshape(bsz * t, D_MODEL), src, axis=0, mode="clip")
    ys = grouped_experts(xs, tile_expert, tile_valid, w_gate, w_up, w_down, layer)
    ya = jnp.take(ys, d0, axis=0, mode="clip").reshape(bsz, t, D_MODEL)
    yb = jnp.take(ys, d1, axis=0, mode="clip").reshape(bsz, t, D_MODEL)
    lanes = lambda r: jnp.broadcast_to(route[:, r, :, None], (bsz, t, V7X_LANES))
    return combine_experts(x_new, ya, yb, lanes(2), lanes(3), mod_tab, layer, n_ctx_tiles)


ATT_TILE = 256
ATT_CHUNK_TILES = 4
Q_SCALE = MLA_QK ** -0.5 * math.log2(math.e)
_ROPE_LO = MLA_NOPE
_ROPE_HALF = MLA_ROPE // 2


def rope_tables(n_ctx, n_lat):
    rows = n_lat // GRID_W
    row = jnp.broadcast_to(jnp.arange(rows, dtype=_F32)[:, None], (rows, GRID_W)).reshape(n_lat)
    col = jnp.broadcast_to(jnp.arange(GRID_W, dtype=_F32)[None, :], (rows, GRID_W)).reshape(n_lat)
    n_f = MLA_ROPE // 4
    inv = ROPE_THETA ** (-jnp.arange(n_f, dtype=_F32) / n_f)
    ang = jnp.concatenate([row[:, None] * inv, col[:, None] * inv], -1)
    cos = jnp.concatenate([jnp.ones((n_ctx, _ROPE_HALF), _F32), jnp.cos(ang)], 0)
    sin = jnp.concatenate([jnp.zeros((n_ctx, _ROPE_HALF), _F32), jnp.sin(ang)], 0)
    t = n_ctx + n_lat
    ones, zeros = jnp.ones((t, MLA_NOPE), _F32), jnp.zeros((t, MLA_NOPE), _F32)
    tail1, tail0 = jnp.ones((t, HEAD_PAD - MLA_QK), _F32), jnp.zeros((t, HEAD_PAD - MLA_QK), _F32)
    cos_p = jnp.concatenate([ones, cos, cos, tail1], -1)
    sin_p = jnp.concatenate([zeros, -sin, sin, tail0], -1)
    return cos.T, sin.T, cos_p, sin_p


def pack_mla_weights(w_uq, w_ukv, qk_norm_q, qk_norm_k):
    depth = w_uq.shape[0]
    pad = HEAD_PAD - MLA_QK
    wq = w_uq.reshape(depth, MLA_Q_RANK, MLA_HEADS, MLA_QK)
    wq = jnp.pad(wq, ((0, 0), (0, 0), (0, 0), (0, pad))).reshape(depth, MLA_Q_RANK, MLA_HEADS * HEAD_PAD)
    wq_t = jnp.swapaxes(wq, 1, 2).astype(_BF16)
    wkv = w_ukv.reshape(depth, MLA_KV_RANK, MLA_HEADS, MLA_NOPE + MLA_V)
    wk = jnp.pad(wkv[..., :MLA_NOPE], ((0, 0), (0, 0), (0, 0), (0, HEAD_PAD - MLA_NOPE)))
    wk = wk.reshape(depth, MLA_KV_RANK, MLA_HEADS * HEAD_PAD).astype(_BF16)
    wv_t = jnp.swapaxes(wkv[..., MLA_NOPE:].reshape(depth, MLA_KV_RANK, MLA_HEADS * MLA_V), 1, 2).astype(_BF16)
    place = jnp.zeros((MLA_ROPE, HEAD_PAD), _F32).at[jnp.arange(MLA_ROPE), _ROPE_LO + jnp.arange(MLA_ROPE)].set(1.0)
    place = jnp.tile(place, (1, MLA_HEADS)).astype(_BF16)
    gq = jnp.pad(qk_norm_q, ((0, 0), (0, pad)))
    gq_col = jnp.broadcast_to(gq[:, :, None], (depth, HEAD_PAD, ATT_TILE)).astype(_F32)
    gk_row = jnp.pad(qk_norm_k, ((0, 0), (0, pad))).reshape(depth, 1, HEAD_PAD).astype(_F32)
    return wq_t, wk, wv_t, place, gq_col, gk_row


def _mla_prep_kernel(ql_ref, kvl_ref, kr_ref, qn_ref, kvn_ref, wq_ref, wk_ref, wv_ref, place_ref, gq_ref, gk_ref,
                     cos_t_ref, sin_t_ref, cos_p_ref, sin_p_ref, qt_ref, k_ref, vt_ref):
    nt = (((1,), (1,)), ((), ()))
    ql = ql_ref[...].astype(_F32)
    qn = (ql * lax.rsqrt(jnp.mean(ql * ql, -1, keepdims=True) + EPS) * qn_ref[...]).astype(_BF16)
    kvl = kvl_ref[...].astype(_F32)
    kvn = (kvl * lax.rsqrt(jnp.mean(kvl * kvl, -1, keepdims=True) + EPS) * kvn_ref[...]).astype(_BF16)

    q_t = lax.dot_general(wq_ref[...], qn, nt, preferred_element_type=_F32)
    cos_t, sin_t = cos_t_ref[...], sin_t_ref[...]
    scale = Q_SCALE
    lo, mid, hi = _ROPE_LO, _ROPE_LO + _ROPE_HALF, _ROPE_LO + MLA_ROPE
    for h in range(MLA_HEADS):
        blk = q_t[h * HEAD_PAD:(h + 1) * HEAD_PAD, :]
        ms = jnp.sum(blk * blk, 0, keepdims=True) * (1.0 / MLA_QK)
        y = blk * (lax.rsqrt(ms + EPS) * scale) * gq_ref[...]
        r1, r2 = y[lo:mid, :], y[mid:hi, :]
        rot = jnp.concatenate([y[:lo, :], r1 * cos_t - r2 * sin_t, r2 * cos_t + r1 * sin_t, y[hi:, :]], 0)
        qt_ref[h] = rot.astype(qt_ref.dtype)

    k_pre = _dot(kvn, wk_ref[...]) + _dot(kr_ref[...], place_ref[...])
    cos_p, sin_p = cos_p_ref[...], sin_p_ref[...]
    lane = lax.broadcasted_iota(jnp.int32, cos_p.shape, 1)
    for h in range(MLA_HEADS):
        blk = k_pre[:, h * HEAD_PAD:(h + 1) * HEAD_PAD]
        ms = jnp.sum(blk * blk, -1, keepdims=True) * (1.0 / MLA_QK)
        y = blk * lax.rsqrt(ms + EPS) * gk_ref[...]
        swap = jnp.where(lane < mid, pltpu.roll(y, HEAD_PAD - _ROPE_HALF, 1), pltpu.roll(y, _ROPE_HALF, 1))
        k_ref[h] = (y * cos_p + swap * sin_p).astype(k_ref.dtype)

    v_t = lax.dot_general(wv_ref[...], kvn, nt, preferred_element_type=_F32)
    for h in range(MLA_HEADS):
        vt_ref[h] = v_t[h * MLA_V:(h + 1) * MLA_V, :].astype(vt_ref.dtype)


def mla_prepare(q_lat, kv_lat, k_rope, q_norm, kv_norm, packed, tables, layer):
    bsz, t, _ = q_lat.shape
    wq_t, wk, wv_t, place, gq_col, gk_row = packed
    cos_t, sin_t, cos_p, sin_p = tables
    n_tiles = t // ATT_TILE
    row = lambda w: pl.BlockSpec((None, ATT_TILE, w), lambda b, i: (b, i, 0))
    lay = lambda *shape: pl.BlockSpec((None,) + shape, lambda b, i: (layer,) + (0,) * len(shape))
    return pl.pallas_call(
        _mla_prep_kernel,
        grid=(bsz, n_tiles),
        in_specs=[
            row(MLA_Q_RANK), row(MLA_KV_RANK), row(MLA_ROPE),
            lay(1, MLA_Q_RANK), lay(1, MLA_KV_RANK),
            lay(MLA_HEADS * HEAD_PAD, MLA_Q_RANK), lay(MLA_KV_RANK, MLA_HEADS * HEAD_PAD),
            lay(MLA_HEADS * MLA_V, MLA_KV_RANK),
            pl.BlockSpec((MLA_ROPE, MLA_HEADS * HEAD_PAD), lambda b, i: (0, 0)),
            lay(HEAD_PAD, ATT_TILE), lay(1, HEAD_PAD),
            pl.BlockSpec((_ROPE_HALF, ATT_TILE), lambda b, i: (0, i)),
            pl.BlockSpec((_ROPE_HALF, ATT_TILE), lambda b, i: (0, i)),
            pl.BlockSpec((ATT_TILE, HEAD_PAD), lambda b, i: (i, 0)),
            pl.BlockSpec((ATT_TILE, HEAD_PAD), lambda b, i: (i, 0)),
        ],
        out_specs=[
            pl.BlockSpec((None, MLA_HEADS, HEAD_PAD, ATT_TILE), lambda b, i: (b, 0, 0, i)),
            pl.BlockSpec((None, MLA_HEADS, ATT_TILE, HEAD_PAD), lambda b, i: (b, 0, i, 0)),
            pl.BlockSpec((None, MLA_HEADS, None, MLA_V, ATT_TILE), lambda b, i: (b, 0, i, 0, 0)),
        ],
        out_shape=[
            jax.ShapeDtypeStruct((bsz, MLA_HEADS, HEAD_PAD, t), _BF16),
            jax.ShapeDtypeStruct((bsz, MLA_HEADS, t, HEAD_PAD), _BF16),
            jax.ShapeDtypeStruct((bsz, MLA_HEADS, n_tiles, MLA_V, ATT_TILE), _BF16),
        ],
        compiler_params=_cparams("parallel", "parallel"),
        name="mla_prepare",
    )(q_lat, kv_lat, k_rope, q_norm, kv_norm, wq_t, wk, wv_t, place, gq_col, gk_row, cos_t, sin_t, cos_p, sin_p)


def _attention_scores(s_ref, slot, q_t, k_ref, tile0, n_sub):
    for j in range(n_sub):
        row0 = (tile0 + j) * ATT_TILE
        if not isinstance(row0, int):
            row0 = pl.multiple_of(row0, ATT_TILE)
        s_ref[slot, j * ATT_TILE:(j + 1) * ATT_TILE, :] = _dot(k_ref[pl.ds(row0, ATT_TILE), :], q_t)


def _attention_update(s_ref, slot, vt_ref, tile0, n_sub, carry):
    m, l, acc = carry
    tiles = [s_ref[slot, j * ATT_TILE:(j + 1) * ATT_TILE, :] for j in range(n_sub)]
    m_new = m
    for s_j in tiles:
        m_new = jnp.maximum(m_new, jnp.max(s_j, 0, keepdims=True))
    alpha = jnp.exp2(m - m_new)
    l = alpha * l
    acc = alpha * acc
    for j, s_j in enumerate(tiles):
        p = jnp.exp2(s_j - m_new)
        l = l + jnp.sum(p, 0, keepdims=True)
        acc = acc + _dot(vt_ref[tile0 + j], p.astype(_BF16))
    return m_new, l, acc


def _attention_kernel(qt_ref, k_ref, vt_ref, o_ref, s_ref, *, n_ctx_tiles, n_tiles, n_sub):
    i = pl.program_id(2)
    q_t = qt_ref[...]
    tq = q_t.shape[1]
    n_chunks = (n_tiles - n_ctx_tiles) // n_sub
    first = lambda c: n_ctx_tiles + c * n_sub
    carry = (jnp.full((1, tq), -jnp.inf, _F32), jnp.zeros((1, tq), _F32), jnp.zeros((MLA_V, tq), _F32))
    _attention_scores(s_ref, 1, q_t, k_ref, 0, n_ctx_tiles)
    carry = _attention_update(s_ref, 1, vt_ref, 0, n_ctx_tiles, carry)
    _attention_scores(s_ref, 0, q_t, k_ref, first(0), n_sub)

    def body(cp, cr):
        c0 = 2 * cp
        _attention_scores(s_ref, 1, q_t, k_ref, first(c0 + 1), n_sub)
        cr = _attention_update(s_ref, 0, vt_ref, first(c0), n_sub, cr)
        _attention_scores(s_ref, 0, q_t, k_ref, first(jnp.minimum(c0 + 2, n_chunks - 1)), n_sub)
        return _attention_update(s_ref, 1, vt_ref, first(c0 + 1), n_sub, cr)

    n_pairs = jnp.where(i < n_ctx_tiles, 0, n_chunks // 2)
    _, l, acc = lax.fori_loop(0, n_pairs, body, carry)
    o_ref[...] = (acc / l).astype(o_ref.dtype)


def flash_attention(q_t, k, v_t, n_ctx_tiles):
    bsz, heads, _, t = q_t.shape
    n_tiles = t // ATT_TILE
    n_lat_tiles = n_tiles - n_ctx_tiles
    assert n_lat_tiles % 2 == 0
    n_sub = math.gcd(n_lat_tiles // 2, ATT_CHUNK_TILES)
    return pl.pallas_call(
        functools.partial(_attention_kernel, n_ctx_tiles=n_ctx_tiles, n_tiles=n_tiles, n_sub=n_sub),
        grid=(bsz, heads, n_tiles),
        in_specs=[
            pl.BlockSpec((None, None, HEAD_PAD, ATT_TILE), lambda b, h, i: (b, h, 0, i)),
            pl.BlockSpec((None, None, t, HEAD_PAD), lambda b, h, i: (b, h, 0, 0)),
            pl.BlockSpec((None, None, n_tiles, MLA_V, ATT_TILE), lambda b, h, i: (b, h, 0, 0, 0)),
        ],
        out_specs=pl.BlockSpec((None, MLA_V, ATT_TILE), lambda b, h, i: (b, h, i)),
        out_shape=jax.ShapeDtypeStruct((bsz, heads * MLA_V, t), _BF16),
        scratch_shapes=[pltpu.VMEM((2, max(n_sub, n_ctx_tiles) * ATT_TILE, ATT_TILE), _F32)],
        compiler_params=_cparams("parallel", "parallel", "arbitrary"),
        name="flash_attention",
    )(q_t, k, v_t)


S5_KG = SSM_W // V7X_LANES
S5_GPK = SSM_GROUPS // S5_KG
S5_HALF = S5_GPK * SSM_STATE
S5_COLS = 2 * SSM_GROUPS * SSM_STATE
S5_PAIRS = SSM_GROUPS * SSM_STATE // V7X_LANES
S5_LEVELS = (1, 2, 4)


def _s5_disc_kernel(lr_ref, li_ref, ldt_ref, br_ref, bi_ref, pr_ref, pi_ref, bbr_ref, bbi_ref):
    lr = jnp.minimum(lr_ref[...], -1e-4)
    li = li_ref[...]
    dt = jnp.exp(ldt_ref[...])
    mag = jnp.exp(lr * dt)
    ar, ai = mag * jnp.cos(li * dt), mag * jnp.sin(li * dt)
    den = lr * lr + li * li
    gr = ((ar - 1.0) * lr + ai * li) / den
    gi = (ai * lr - (ar - 1.0) * li) / den
    br, bi = br_ref[...], bi_ref[...]
    bbr_ref[...] = gr * br - gi * bi
    bbi_ref[...] = gr * bi + gi * br
    pr, pi = ar, ai
    for k in range(V7X_SUBLANES):
        pr_ref[k] = pr
        pi_ref[k] = pi
        pr, pi = pr * ar - pi * ai, pr * ai + pi * ar


def s5_discretize(lam_re, lam_im, log_dt, b_re, b_im):
    depth = lam_re.shape[0]
    ld = depth * 2
    g, p, j = SSM_GROUPS, SSM_STATE, SSM_GROUP
    flat = lambda a: a.reshape((ld,) + a.shape[2:])
    g1p = lambda a: a.reshape(ld, g, 1, p)
    ldt = jnp.broadcast_to(flat(log_dt)[:, :, None, None], (ld, g, 1, p))
    b_t = lambda a: jnp.swapaxes(flat(a), 2, 3)
    gp = pl.BlockSpec((None, g, 1, p), lambda i: (i, 0, 0, 0))
    gjp = pl.BlockSpec((None, g, j, p), lambda i: (i, 0, 0, 0))
    pw = pl.BlockSpec((None, V7X_SUBLANES, g, 1, p), lambda i: (i, 0, 0, 0, 0))
    pow_r, pow_i, bbr, bbi = pl.pallas_call(
        _s5_disc_kernel,
        grid=(ld,),
        in_specs=[gp, gp, gp, gjp, gjp],
        out_specs=[pw, pw, gjp, gjp],
        out_shape=[jax.ShapeDtypeStruct((ld, V7X_SUBLANES, g, 1, p), _F32)] * 2
        + [jax.ShapeDtypeStruct((ld, g, j, p), _F32)] * 2,
        compiler_params=_cparams("parallel"),
        name="s5_discretize",
    )(g1p(lam_re), g1p(lam_im), ldt, b_t(b_re), b_t(b_im))
    return pow_r.reshape(ld, V7X_SUBLANES, g, p), pow_i.reshape(ld, V7X_SUBLANES, g, p), bbr, bbi


def pack_s5(pow_r, pow_i, bbr, bbi, c_re, c_im):
    ld = pow_r.shape[0]
    n = SSM_GROUPS * SSM_STATE
    rev = (jnp.arange(ld) % 2 == 1)[:, None, None]
    t = jnp.arange(V7X_SUBLANES)[None, :, None]
    pr, pi = pow_r.reshape(ld, V7X_SUBLANES, n), pow_i.reshape(ld, V7X_SUBLANES, n)
    consts = []
    for s in S5_LEVELS:
        keep = jnp.where(rev, t < V7X_SUBLANES - s, t >= s)
        for a in (pr, pi):
            consts.append(jnp.where(keep, a[:, s - 1:s, :], 0.0))
    for a in (pr, pi):
        consts.append(jnp.where(rev, a[:, ::-1, :], a))
    consts = jnp.stack(consts, 1)
    eye = jnp.eye(S5_GPK, dtype=_F32)
    bb = jnp.stack([bbr, bbi], 1).reshape(ld, 2, S5_KG, S5_GPK, SSM_GROUP, SSM_STATE)
    bd = jnp.einsum("dckgjp,gh->dkgjchp", bb, eye).reshape(ld, S5_KG, V7X_LANES, 2 * S5_HALF).astype(_BF16)
    cc = jnp.stack([c_re, -c_im], 2)
    cc = cc.reshape(ld, 2, S5_KG, S5_GPK, SSM_GROUP, SSM_STATE)
    cd = jnp.einsum("dckgjp,gh->dkcgphj", cc, eye).reshape(ld, S5_KG, 2 * S5_HALF, V7X_LANES).astype(_BF16)
    return consts, bd, cd


def _s5_scan_kernel(u_ref, bd_ref, cd_ref, k_ref, *rest, reverse, finish):
    if finish:
        yf_ref, d_ref, wg_ref, bg_ref, o_ref, bu_ref, carry_ref = rest
    else:
        o_ref, bu_ref, carry_ref = rest
    n_rows = u_ref.shape[0]

    @pl.when(pl.program_id(1) == 0)
    def _():
        carry_ref[...] = jnp.zeros_like(carry_ref)

    u = u_ref[...]
    for kg in range(S5_KG):
        bu_ref[:, kg * 2 * S5_HALF:(kg + 1) * 2 * S5_HALF] = _dot(u[:, kg * V7X_LANES:(kg + 1) * V7X_LANES], bd_ref[kg])

    n_vregs = n_rows // V7X_SUBLANES
    last = 0 if reverse else V7X_SUBLANES - 1

    def step(r, _):
        row = pl.multiple_of((n_vregs - 1 - r if reverse else r) * V7X_SUBLANES, V7X_SUBLANES)
        for jp in range(S5_PAIRS):
            kg, q = divmod(jp, S5_PAIRS // S5_KG)
            cr = kg * 2 * S5_HALF + q * V7X_LANES
            ci = cr + S5_HALF
            cl = jp * V7X_LANES
            xr = bu_ref[pl.ds(row, V7X_SUBLANES), cr:cr + V7X_LANES]
            xi = bu_ref[pl.ds(row, V7X_SUBLANES), ci:ci + V7X_LANES]
            for lvl, s in enumerate(S5_LEVELS):
                shift = V7X_SUBLANES - s if reverse else s
                sr, si = pltpu.roll(xr, shift, 0), pltpu.roll(xi, shift, 0)
                ar = k_ref[2 * lvl, :, cl:cl + V7X_LANES]
                ai = k_ref[2 * lvl + 1, :, cl:cl + V7X_LANES]
                xr, xi = xr + (ar * sr - ai * si), xi + (ar * si + ai * sr)
            pr = k_ref[2 * len(S5_LEVELS), :, cl:cl + V7X_LANES]
            pi = k_ref[2 * len(S5_LEVELS) + 1, :, cl:cl + V7X_LANES]
            hr0 = carry_ref[:, cr:cr + V7X_LANES]
            hi0 = carry_ref[:, ci:ci + V7X_LANES]
            xr, xi = xr + (pr * hr0 - pi * hi0), xi + (pr * hi0 + pi * hr0)
            bu_ref[pl.ds(row, V7X_SUBLANES), cr:cr + V7X_LANES] = xr
            bu_ref[pl.ds(row, V7X_SUBLANES), ci:ci + V7X_LANES] = xi
            carry_ref[:, cr:cr + V7X_LANES] = jnp.broadcast_to(xr[last:last + 1, :], xr.shape)
            carry_ref[:, ci:ci + V7X_LANES] = jnp.broadcast_to(xi[last:last + 1, :], xi.shape)
        return 0

    lax.fori_loop(0, n_vregs, step, 0)

    ys = [_dot(bu_ref[:, kg * 2 * S5_HALF:(kg + 1) * 2 * S5_HALF].astype(_BF16), cd_ref[kg]) for kg in range(S5_KG)]
    y = jnp.concatenate(ys, -1)
    if finish:
        y = y + yf_ref[...] + d_ref[...] * u.astype(_F32)
        z = jax.nn.gelu(y)
        o_ref[...] = (z * jax.nn.sigmoid(_dot(z.astype(_BF16), wg_ref[...]) + bg_ref[...])).astype(o_ref.dtype)
    else:
        o_ref[...] = y


def s5_scan(u, consts, bd, cd, layer, n_ctx_tiles, reverse, finish_args=None):
    bsz, t, _ = u.shape
    n_tiles = t // ROW_TILE
    d = 2 * layer + (1 if reverse else 0)
    if reverse:
        chunk = lambda i: jnp.where(i < n_ctx_tiles, n_ctx_tiles - 1 - i, n_tiles - 1 - (i - n_ctx_tiles))
    else:
        chunk = lambda i: i
    row = lambda w: pl.BlockSpec((None, ROW_TILE, w), lambda b, i: (b, chunk(i), 0))
    lay = lambda *shape: pl.BlockSpec((None,) + shape, lambda b, i: (d,) + (0,) * len(shape))
    in_specs = [row(SSM_W), lay(S5_KG, V7X_LANES, 2 * S5_HALF), lay(S5_KG, 2 * S5_HALF, V7X_LANES),
                lay(2 * len(S5_LEVELS) + 2, V7X_SUBLANES, SSM_GROUPS * SSM_STATE)]
    args = [u, bd, cd, consts]
    finish = finish_args is not None
    if finish:
        y_fwd, d_skip, w_glu, b_glu = finish_args
        lyr = lambda *shape: pl.BlockSpec((None,) + shape, lambda b, i: (layer,) + (0,) * len(shape))
        in_specs += [row(SSM_W), lyr(1, SSM_W), lyr(SSM_W, SSM_W), lyr(1, SSM_W)]
        args += [y_fwd, d_skip, w_glu, b_glu]
    return pl.pallas_call(
        functools.partial(_s5_scan_kernel, reverse=reverse, finish=finish),
        grid=(bsz, n_tiles),
        in_specs=in_specs,
        out_specs=row(SSM_W),
        out_shape=jax.ShapeDtypeStruct((bsz, t, SSM_W), _BF16 if finish else _F32),
        scratch_shapes=[pltpu.VMEM((ROW_TILE, S5_COLS), _F32), pltpu.VMEM((V7X_SUBLANES, S5_COLS), _F32)],
        compiler_params=_cparams("parallel", "arbitrary"),
        name="s5_scan_bwd" if reverse else "s5_scan_fwd",
    )(*args)


HY_CT = 512
HY_FEAT_PAD = V7X_LANES
HY_TAP_COLS = HY_ORDER * HY_W


def _short_conv_kernel(prev_ref, u_ref, next_ref, w_ref, b_ref, *out_refs, n_ctx, n_total):
    i = pl.program_id(1)
    u = u_ref[...].astype(_F32)
    rows = u.shape[0]
    r = lax.broadcasted_iota(jnp.int32, u.shape, 0)
    g = r + i * rows
    before = jnp.broadcast_to(prev_ref[V7X_SUBLANES - 1:V7X_SUBLANES, :].astype(_F32), u.shape)
    after = jnp.broadcast_to(next_ref[0:1, :].astype(_F32), u.shape)
    up = jnp.where(r == 0, before, pltpu.roll(u, 1, 0))
    un = jnp.where(r == rows - 1, after, pltpu.roll(u, rows - 1, 0))
    up = jnp.where((g == 0) | (g == n_ctx), 0.0, up)
    un = jnp.where((g == n_ctx - 1) | (g == n_total - 1), 0.0, un)
    us = w_ref[0:1, :] * up + w_ref[1:2, :] * u + w_ref[2:3, :] * un + b_ref[...]
    parts = [us[:, k * HY_W:(k + 1) * HY_W] for k in range(3)]

    @pl.when(i * rows < n_ctx)
    def _():
        for ref, part in zip(out_refs[:3], parts):
            ref[...] = part.astype(ref.dtype)

    @pl.when(i * rows >= n_ctx)
    def _():
        for ref, part in zip(out_refs[3:], parts):
            ref[...] = part.astype(ref.dtype)


def hyena_short_conv(hy, conv_w, conv_b, layer, n_ctx):
    bsz, t, width = hy.shape
    per = ROW_TILE // V7X_SUBLANES
    n_halo = t // V7X_SUBLANES
    n_ctx_tiles = n_ctx // ROW_TILE
    row = lambda w: pl.BlockSpec((None, ROW_TILE, w), lambda b, i: (b, i, 0))
    return pl.pallas_call(
        functools.partial(_short_conv_kernel, n_ctx=n_ctx, n_total=t),
        grid=(bsz, t // ROW_TILE),
        in_specs=[
            pl.BlockSpec((None, V7X_SUBLANES, width), lambda b, i: (b, jnp.maximum(i * per - 1, 0), 0)),
            row(width),
            pl.BlockSpec((None, V7X_SUBLANES, width), lambda b, i: (b, jnp.minimum((i + 1) * per, n_halo - 1), 0)),
            pl.BlockSpec((None, conv_w.shape[1], width), lambda b, i: (layer, 0, 0)),
            pl.BlockSpec((None, 1, width), lambda b, i: (layer, 0, 0)),
        ],
        out_specs=[pl.BlockSpec((None, ROW_TILE, HY_W), lambda b, i: (b, jnp.minimum(i, n_ctx_tiles - 1), 0))] * 3
        + [pl.BlockSpec((None, ROW_TILE, HY_W), lambda b, i: (b, jnp.maximum(i - n_ctx_tiles, 0), 0))] * 3,
        out_shape=[jax.ShapeDtypeStruct((bsz, n_ctx, HY_W), _BF16)] * 3
        + [jax.ShapeDtypeStruct((bsz, t - n_ctx, HY_W), _BF16)] * 3,
        compiler_params=_cparams("arbitrary", "arbitrary"),
        name="hyena_short_conv",
    )(hy, hy, hy, conv_w, conv_b)


def hyena_features(n):
    m = jnp.arange(2 * n, dtype=jnp.int32)
    t = jnp.where(m < n, m, jnp.where(m == n, 0, 2 * n - m)).astype(_F32)
    t_norm = t / (n - 1)
    bands = jnp.linspace(1e-4, HY_BANDS - 1, HY_BANDS, dtype=_F32)
    ang = (2.0 * math.pi * t / n)[:, None] * bands[None, :]
    feat = jnp.concatenate([t_norm[:, None], jnp.cos(ang), -jnp.sin(ang)], -1)
    feat = jnp.pad(feat, ((0, 0), (0, HY_FEAT_PAD - HY_EMB)))
    return feat, jnp.broadcast_to(t_norm[:, None], (2 * n, V7X_LANES))


def _filter_kernel(feat_ref, tn_ref, w1_ref, b1_ref, w2_ref, b2_ref, w3_ref, fr_ref, dec_ref, h_ref, l1_ref, *, n):
    i = pl.program_id(0)
    hp = lax.Precision.HIGHEST
    fr = fr_ref[...]
    h = jnp.sin(fr * (jnp.dot(feat_ref[...], w1_ref[...], preferred_element_type=_F32, precision=hp) + b1_ref[...]))
    h = jnp.sin(fr * (jnp.dot(h, w2_ref[...], preferred_element_type=_F32, precision=hp) + b2_ref[...]))
    h = _dot(h.astype(_BF16), w3_ref[...])
    tn = tn_ref[...]
    rows = h.shape[0]
    zero_tap = (lax.broadcasted_iota(jnp.int32, (rows, V7X_LANES), 0) + i * rows) == n

    @pl.when(i == 0)
    def _():
        l1_ref[...] = jnp.zeros_like(l1_ref)

    for c in range(HY_TAP_COLS // V7X_LANES):
        sl = slice(c * V7X_LANES, (c + 1) * V7X_LANES)
        blk = jnp.where(zero_tap, 0.0, h[:, sl] * jnp.exp(-tn * jnp.abs(dec_ref[:, sl])))
        h_ref[:, sl] = blk.astype(h_ref.dtype)
        l1_ref[:, sl] += jnp.sum(jnp.abs(blk).reshape(rows // V7X_SUBLANES, V7X_SUBLANES, V7X_LANES), 0)


def hyena_filter_taps(n, w1, b1, w2, b2, w3_dir, freq, decay_dir, layer):
    feat, tn = hyena_features(n)
    rows = min(ROW_TILE, n)
    half = n // rows
    lay = lambda *shape: pl.BlockSpec((None,) + shape, lambda i: (layer,) + (0,) * len(shape))
    by_dir = lambda *shape: pl.BlockSpec((None, None) + shape,
                                         lambda i: (layer, jnp.where(i >= half, 1, 0)) + (0,) * len(shape))
    return pl.pallas_call(
        functools.partial(_filter_kernel, n=n),
        grid=(2 * half,),
        in_specs=[
            pl.BlockSpec((rows, HY_FEAT_PAD), lambda i: (i, 0)),
            pl.BlockSpec((rows, V7X_LANES), lambda i: (i, 0)),
            lay(HY_FEAT_PAD, HY_HIDDEN), lay(1, HY_HIDDEN), lay(HY_HIDDEN, HY_HIDDEN), lay(1, HY_HIDDEN),
            by_dir(HY_HIDDEN, HY_TAP_COLS), lay(1, HY_HIDDEN), by_dir(1, HY_TAP_COLS),
        ],
        out_specs=[pl.BlockSpec((rows, HY_TAP_COLS), lambda i: (i, 0)),
                   pl.BlockSpec((V7X_SUBLANES, HY_TAP_COLS), lambda i: (0, 0))],
        out_shape=[jax.ShapeDtypeStruct((2 * n, HY_TAP_COLS), _BF16),
                   jax.ShapeDtypeStruct((V7X_SUBLANES, HY_TAP_COLS), _F32)],
        compiler_params=_cparams("arbitrary"),
        name="hyena_filter_taps",
    )(feat, tn, w1, b1, w2, b2, w3_dir, freq, decay_dir)


def _dft_block(n_out, n_in, sign, scale=1.0, real_input=False):
    size = max(n_out, n_in)
    ang = 2.0 * np.pi * np.outer(np.arange(n_out), np.arange(n_in)) / size
    fr, fi = np.cos(ang) * scale, sign * np.sin(ang) * scale
    blk = np.concatenate([fr, fi], 0) if real_input else np.block([[fr, -fi], [fi, fr]])
    return jnp.asarray(blk, _F32).astype(_BF16)


def _twiddle_table(n1, n2, sign):
    ang = 2.0 * np.pi * np.outer(np.arange(n1), np.arange(n2)) / (n1 * n2)
    tw = np.stack([np.cos(ang), sign * np.sin(ang)], 0)[..., None]
    return jnp.broadcast_to(jnp.asarray(tw, _F32), (2, n1, n2, V7X_LANES))


def _cmul(ar, ai, br, bi):
    return ar * br - ai * bi, ar * bi + ai * br


def _lanes_mul(xr, xi, twr, twi):
    outs_r, outs_i = [], []
    for c in range(xr.shape[1] // V7X_LANES):
        sl = slice(c * V7X_LANES, (c + 1) * V7X_LANES)
        r, i = _cmul(xr[:, sl], xi[:, sl], twr, twi)
        outs_r.append(r)
        outs_i.append(i)
    return jnp.concatenate(outs_r, 1), jnp.concatenate(outs_i, 1)


def _dft_kernel(*refs, real_input, pre_tw, has_filter, has_second, post_tw, slabs):
    refs = list(refs)
    xr_ref = refs.pop(0)
    xi_ref = None if real_input else refs.pop(0)
    f_ref = refs.pop(0)
    tw_ref = refs.pop(0) if (pre_tw or post_tw) else None
    if has_filter:
        hr_ref, hi_ref, l1_ref = refs.pop(0), refs.pop(0), refs.pop(0)
    g_ref = refs.pop(0) if has_second else None
    or_ref, oi_ref = refs
    for a in range(slabs):
        xr = xr_ref[a].astype(_F32)
        if real_input:
            s = xr.astype(_BF16)
        else:
            xi = xi_ref[a].astype(_F32)
            if pre_tw:
                xr, xi = _lanes_mul(xr, xi, tw_ref[0, a], tw_ref[1, a])
            s = jnp.concatenate([xr, xi], 0).astype(_BF16)
        y = _dot(f_ref[...], s)
        m = y.shape[0] // 2
        yr, yi = y[:m], y[m:]
        if has_filter:
            inv = 1.0 / (jnp.sum(l1_ref[...], 0, keepdims=True) + EPS)
            yr, yi = _cmul(yr, yi, hr_ref[a].astype(_F32) * inv, hi_ref[a].astype(_F32) * inv)
        if has_second:
            z = _dot(g_ref[...], jnp.concatenate([yr, yi], 0).astype(_BF16))
            m = z.shape[0] // 2
            yr, yi = z[:m], z[m:]
        if post_tw:
            yr, yi = _lanes_mul(yr, yi, tw_ref[0, a], -tw_ref[1, a])
        or_ref[a] = yr.astype(or_ref.dtype)
        oi_ref[a] = yi.astype(oi_ref.dtype)


def dft_apply(xr, xi, f_blk, *, tw=None, pre_tw=False, post_tw=False, filt=None, g_blk=None, out_dtype=_BF16):
    n_a, k, c = xr.shape
    real_input = xi is None
    m1 = f_blk.shape[0] // 2
    m_out = g_blk.shape[0] // 2 if g_blk is not None else m1
    ct = min(HY_CT, c)
    slabs = max(1, min(n_a, 1024 // max(k, m_out)))
    slab_spec = lambda rows: pl.BlockSpec((slabs, rows, ct), lambda j, a: (a, 0, j))
    whole = lambda arr: pl.BlockSpec(arr.shape, lambda j, a: (0,) * arr.ndim)
    in_specs, args = [slab_spec(k)], [xr]
    if not real_input:
        in_specs.append(slab_spec(k))
        args.append(xi)
    in_specs.append(whole(f_blk))
    args.append(f_blk)
    if pre_tw or post_tw:
        in_specs.append(pl.BlockSpec((2, slabs, tw.shape[2], V7X_LANES), lambda j, a: (0, a, 0, 0)))
        args.append(tw)
    if filt is not None:
        h_re, h_im, l1, first_block = filt
        h_spec = pl.BlockSpec((slabs, m1, ct), lambda j, a: (a, 0, j + first_block))
        l1_spec = pl.BlockSpec((V7X_SUBLANES, ct), lambda j, a: (0, j + first_block))
        in_specs += [h_spec, h_spec, l1_spec]
        args += [h_re, h_im, l1]
    if g_blk is not None:
        in_specs.append(whole(g_blk))
        args.append(g_blk)
    kern = functools.partial(_dft_kernel, real_input=real_input, pre_tw=pre_tw, has_filter=filt is not None,
                             has_second=g_blk is not None, post_tw=post_tw, slabs=slabs)
    return pl.pallas_call(
        kern,
        grid=(c // ct, n_a // slabs),
        in_specs=in_specs,
        out_specs=[slab_spec(m_out)] * 2,
        out_shape=[jax.ShapeDtypeStruct((n_a, m_out, c), out_dtype)] * 2,
        compiler_params=_cparams("parallel", "parallel"),
        name="hyena_dft",
    )(*args)


def _fft_factors(n_fft):
    n1 = 1 << ((n_fft.bit_length() - 1) // 2)
    return n_fft // n1, n1


def filter_spectrum(taps):
    n_fft, c = taps.shape
    if n_fft <= 512:
        return dft_apply(taps[None], None, _dft_block(n_fft, n_fft, -1.0, real_input=True))
    n1, n2 = _fft_factors(n_fft)
    x = jnp.swapaxes(taps.reshape(n1, n2, c), 0, 1)
    yr, yi = dft_apply(x, None, _dft_block(n1, n1, -1.0, real_input=True))
    yr, yi = jnp.swapaxes(yr, 0, 1), jnp.swapaxes(yi, 0, 1)
    return dft_apply(yr, yi, _dft_block(n2, n2, -1.0), tw=_twiddle_table(n1, n2, -1.0), pre_tw=True)


def long_conv_pair(vr, vi, filt):
    n, c = vr.shape
    n_fft = 2 * n
    if n_fft <= 512:
        f = _dft_block(n_fft, n, -1.0)
        g = _dft_block(n, n_fft, 1.0, scale=1.0 / n_fft)
        yr, yi = dft_apply(vr[None], vi[None], f, filt=filt, g_blk=g)
        return yr[0], yi[0]
    n1, n2 = _fft_factors(n_fft)
    n1h = n1 // 2
    to_slabs = lambda a: jnp.swapaxes(a.reshape(n1h, n2, c), 0, 1)
    yr, yi = dft_apply(to_slabs(vr), to_slabs(vi), _dft_block(n1, n1h, -1.0))
    yr, yi = jnp.swapaxes(yr, 0, 1), jnp.swapaxes(yi, 0, 1)
    tw = _twiddle_table(n1, n2, -1.0)
    zr, zi = dft_apply(yr, yi, _dft_block(n2, n2, -1.0), tw=tw, pre_tw=True, post_tw=True, filt=filt,
                       g_blk=_dft_block(n2, n2, 1.0))
    zr, zi = jnp.swapaxes(zr, 0, 1), jnp.swapaxes(zi, 0, 1)
    outr, outi = dft_apply(zr, zi, _dft_block(n1h, n1, 1.0, scale=1.0 / n_fft))
    back = lambda a: jnp.swapaxes(a, 0, 1).reshape(n, c)
    return back(outr), back(outi)


def _gate_kernel(x_ref, conv_ref, v_ref, skip_ref, o_ref):
    v = v_ref[...].astype(_F32)
    o_ref[...] = (x_ref[...].astype(_F32) * (conv_ref[...].astype(_F32) + skip_ref[...] * v)).astype(o_ref.dtype)


def hyena_gate(xg, conv, v, skip, layer, order):
    bsz, n, _ = xg.shape
    rows = min(ROW_TILE, n)
    row = pl.BlockSpec((None, rows, HY_W), lambda b, i: (b, i, 0))
    return pl.pallas_call(
        _gate_kernel,
        grid=(bsz, n // rows),
        in_specs=[row, row, row, pl.BlockSpec((None, None, 1, HY_W), lambda b, i: (layer, order, 0, 0))],
        out_specs=row,
        out_shape=jax.ShapeDtypeStruct((bsz, n, HY_W), _BF16),
        compiler_params=_cparams("parallel", "parallel"),
        name="hyena_gate",
    )(xg, conv, v, skip)


def hyena_segment(v, x1, x2, w1, b1, w2, b2, w3, freq, decay, skip, layer):
    bsz, n, _ = v.shape
    assert bsz == 2, "the two batch elements are packed into one complex transform"
    taps, l1 = hyena_filter_taps(n, w1, b1, w2, b2, w3, freq, decay, layer)
    hr, hi = filter_spectrum(taps)
    u = v
    for o, xg in enumerate((x1, x2)):
        filt = (hr, hi, l1, o * (HY_W // min(HY_CT, HY_W)))
        c0, c1 = long_conv_pair(u[0], u[1], filt)
        u = hyena_gate(xg, jnp.stack([c0, c1], 0), u, skip, layer, o)
    return u


def pack_hyena_weights(p):
    depth = p["hy_w1"].shape[0]
    w1 = jnp.pad(p["hy_w1"], ((0, 0), (0, HY_FEAT_PAD - HY_EMB), (0, 0)))
    row = lambda a: a.reshape(depth, 1, -1)
    w3 = p["hy_w3"].reshape(depth, HY_HIDDEN, HY_ORDER, 2, HY_W)
    w3_dir = jnp.transpose(w3, (0, 3, 1, 2, 4)).reshape(depth, 2, HY_HIDDEN, HY_TAP_COLS).astype(_BF16)
    decay_dir = jnp.swapaxes(p["hy_decay"], 1, 2).reshape(depth, 2, 1, HY_TAP_COLS)
    return (w1, row(p["hy_b1"]), p["hy_w2"], row(p["hy_b2"]), w3_dir, row(p["hy_freq"]), decay_dir,
            p["hy_skip"].reshape(depth, HY_ORDER, 1, HY_W))


def kernel(x, c, ctx, c_ctx, w_mod, b_mod, norm1, norm2, w_in, ssm_lam_re, ssm_lam_im, ssm_log_dt, ssm_b_re, ssm_b_im, ssm_c_re, ssm_c_im, ssm_d, ssm_w_glu, ssm_b_glu, mla_q_norm, mla_w_uq, mla_kv_norm, mla_w_ukv, qk_norm_q, qk_norm_k, hy_conv_w, hy_conv_b, hy_w1, hy_b1, hy_w2, hy_b2, hy_w3, hy_freq, hy_decay, hy_skip, w_branch, w_out, router_w, router_bias, moe_w_gate, moe_w_up, moe_w_down):
    depth = w_mod.shape[0]
    bsz, n_lat, _ = x.shape
    n_ctx = ctx.shape[1]
    assert n_ctx % ROW_TILE == 0 and n_lat % ROW_TILE == 0 and ROW_TILE == ATT_TILE
    n_ctx_tiles = n_ctx // ROW_TILE
    row = lambda a: a.reshape(depth, 1, -1)

    mod_tab = modulation_table(c, c_ctx, w_mod, b_mod)
    w_in_used = pack_w_in(w_in)
    w_gates = w_in[:, :, _IN_GATES:].astype(_BF16)
    w_branch_b, w_out_b = w_branch.astype(_BF16), w_out.astype(_BF16)
    moe_b = (moe_w_gate.astype(_BF16), moe_w_up.astype(_BF16), moe_w_down.astype(_BF16))
    rw_hi, rw_lo, rb_col = split_router(router_w, router_bias)
    norm1_r, norm2_r = row(norm1), row(norm2)
    s5_consts, s5_bd, s5_cd = pack_s5(*s5_discretize(ssm_lam_re, ssm_lam_im, ssm_log_dt, ssm_b_re, ssm_b_im),
                                      ssm_c_re, ssm_c_im)
    s5_finish = (row(ssm_d), ssm_w_glu.astype(_BF16), row(ssm_b_glu))
    mla_packed = pack_mla_weights(mla_w_uq, mla_w_ukv, qk_norm_q, qk_norm_k)
    mla_tables = rope_tables(n_ctx, n_lat)
    hy_w = pack_hyena_weights(dict(hy_w1=hy_w1, hy_b1=hy_b1, hy_w2=hy_w2, hy_b2=hy_b2, hy_w3=hy_w3, hy_freq=hy_freq,
                                   hy_decay=hy_decay, hy_skip=hy_skip))
    hy_conv_b_r = row(hy_conv_b)

    xcat = jnp.concatenate([ctx, x], 1)
    for l in range(depth):
        u, q_lat, kv_lat, hy, k_rope = input_projection(xcat, mod_tab, norm1_r, w_in_used, l, n_ctx_tiles)

        y_fwd = s5_scan(u, s5_consts, s5_bd, s5_cd, l, n_ctx_tiles, False)
        y_ssm = s5_scan(u, s5_consts, s5_bd, s5_cd, l, n_ctx_tiles, True, (y_fwd,) + s5_finish)

        q_t, k, v_t = mla_prepare(q_lat, kv_lat, k_rope, row(mla_q_norm), row(mla_kv_norm), mla_packed, mla_tables, l)
        y_att = jnp.swapaxes(flash_attention(q_t, k, v_t, n_ctx_tiles), 1, 2)

        v_c, x1_c, x2_c, v_l, x1_l, x2_l = hyena_short_conv(hy, hy_conv_w, hy_conv_b_r, l, n_ctx)
        hy_lat = hyena_segment(v_l, x1_l, x2_l, *hy_w, l)
        if l < depth - 1:
            hy_ctx = hyena_segment(v_c, x1_c, x2_c, *hy_w, l)
        else:
            hy_ctx = jnp.zeros((bsz, n_ctx, HY_W), hy_lat.dtype)

        x_new, hl, route = merge_and_route(xcat, y_ssm, y_att, hy_ctx, hy_lat, mod_tab, norm1_r, norm2_r, w_gates,
                                           w_branch_b, w_out_b, rw_hi, rw_lo, rb_col, l, n_ctx_tiles)
        xcat = moe_block(x_new, hl, route, mod_tab, *moe_b, l, n_ctx_tiles)
    return xcat[:, n_ctx:]
```

```python
import functools
import math

import numpy as np
import jax
import jax.numpy as jnp
from jax import lax
from jax.experimental import pallas as pl
from jax.experimental.pallas import tpu as pltpu

D_MODEL = 1024
EPS = 1e-6
GRID_W = 64
SSM_W = 512
SSM_GROUP = 16
SSM_GROUPS = SSM_W // SSM_GROUP
SSM_STATE = 64
MLA_HEADS = 8
MLA_Q_RANK = 384
MLA_KV_RANK = 256
MLA_NOPE = 64
MLA_ROPE = 32
MLA_V = 64
MLA_QK = MLA_NOPE + MLA_ROPE
ROPE_THETA = 10000.0
HY_W = 512
HY_ORDER = 2
HY_BANDS = 16
HY_EMB = 1 + 2 * HY_BANDS
HY_HIDDEN = 64
N_BRANCH = 3
IN_SPLITS = (SSM_W, MLA_Q_RANK, MLA_KV_RANK, MLA_ROPE, 3 * HY_W)
N_EXPERTS = 16
N_EXPERT_GROUPS = 4
EXPERTS_PER_GROUP = N_EXPERTS // N_EXPERT_GROUPS
D_EXPERT = 512

V7X_LANES = 128
V7X_SUBLANES = 8
V7X_VMEM_LIMIT = 56 * 1024 * 1024

ROW_TILE = 256
MOE_TILE = 256
HEAD_PAD = 128

_F32 = jnp.float32
_BF16 = jnp.bfloat16


def _cparams(*sem):
    return pltpu.CompilerParams(dimension_semantics=sem, vmem_limit_bytes=V7X_VMEM_LIMIT)


def _silu(v):
    return v * jax.nn.sigmoid(v)


def _modulated_norm(x, g, shift, scale):
    y = x * lax.rsqrt(jnp.mean(x * x, -1, keepdims=True) + EPS)
    return (y * g) * (1.0 + scale) + shift


def _dot(a, b):
    return jnp.dot(a, b, preferred_element_type=_F32)


def _mod_kernel(c_ref, w_ref, b_ref, o_ref):
    s = _silu(c_ref[...])
    o_ref[...] = jnp.dot(s, w_ref[...], preferred_element_type=_F32, precision=lax.Precision.HIGHEST) + b_ref[...]


def modulation_table(c, c_ctx, w_mod, b_mod):
    depth = w_mod.shape[0]
    bsz = c.shape[0]
    rows = V7X_SUBLANES * pl.cdiv(bsz + 1, V7X_SUBLANES)
    cvec = jnp.concatenate([c, c_ctx[None, :], jnp.zeros((rows - bsz - 1, D_MODEL), _F32)], 0)
    out = pl.pallas_call(
        _mod_kernel,
        grid=(depth, 6),
        in_specs=[
            pl.BlockSpec((rows, D_MODEL), lambda l, j: (0, 0)),
            pl.BlockSpec((None, D_MODEL, D_MODEL), lambda l, j: (l, 0, j)),
            pl.BlockSpec((None, 1, D_MODEL), lambda l, j: (l, 0, j)),
        ],
        out_specs=pl.BlockSpec((None, rows, D_MODEL), lambda l, j: (l, 0, j)),
        out_shape=jax.ShapeDtypeStruct((depth, rows, 6 * D_MODEL), _F32),
        compiler_params=_cparams("parallel", "parallel"),
        name="modulation",
    )(cvec, w_mod, b_mod.reshape(depth, 1, 6 * D_MODEL))
    m = out.reshape(depth, rows, 6, D_MODEL)
    lat = m[:, :bsz]
    ctx = jnp.broadcast_to(m[:, bsz:bsz + 1], lat.shape)
    tab = jnp.stack([ctx, lat], 2)
    return jnp.pad(tab, ((0, 0), (0, 0), (0, 0), (0, V7X_SUBLANES - 6), (0, 0)))


def _mod_spec(layer, n_ctx_tiles):
    return pl.BlockSpec((None, None, None, V7X_SUBLANES, D_MODEL),
                        lambda b, i: (layer, b, jnp.where(i >= n_ctx_tiles, 1, 0), 0, 0))


_IN_OFF = (0, SSM_W, SSM_W + MLA_Q_RANK, SSM_W + MLA_Q_RANK + MLA_KV_RANK)
_IN_HY = _IN_OFF[3]
_IN_ROPE = _IN_HY + 3 * HY_W
_IN_USED = _IN_ROPE + MLA_ROPE
_IN_GATES = sum(IN_SPLITS)


def pack_w_in(w_in):
    a = SSM_W + MLA_Q_RANK + MLA_KV_RANK
    return jnp.concatenate([w_in[:, :, :a], w_in[:, :, a + MLA_ROPE:_IN_GATES], w_in[:, :, a:a + MLA_ROPE]],
                           -1).astype(_BF16)


def split_router(router_w, router_bias):
    rw = router_w.T
    hi = rw.astype(_BF16)
    lo = (rw - hi.astype(_F32)).astype(_BF16)
    return hi, lo, router_bias.reshape(N_EXPERTS, 1).astype(_F32)


def _inproj_kernel(x_ref, mod_ref, g_ref, w_ref, u_ref, q_ref, kv_ref, hy_ref, kr_ref):
    xn = _modulated_norm(x_ref[...], g_ref[...], mod_ref[0:1, :], mod_ref[1:2, :]).astype(_BF16)
    u_ref[...] = _dot(xn, w_ref[:, _IN_OFF[0]:_IN_OFF[1]]).astype(u_ref.dtype)
    q_ref[...] = _dot(xn, w_ref[:, _IN_OFF[1]:_IN_OFF[2]]).astype(q_ref.dtype)
    kv_ref[...] = _dot(xn, w_ref[:, _IN_OFF[2]:_IN_OFF[3]]).astype(kv_ref.dtype)
    hy_ref[...] = _dot(xn, w_ref[:, _IN_HY:_IN_ROPE]).astype(hy_ref.dtype)
    kr_ref[...] = _dot(xn, w_ref[:, _IN_ROPE:_IN_USED]).astype(kr_ref.dtype)


def input_projection(x, mod_tab, norm1, w_in_used, layer, n_ctx_tiles):
    bsz, t, _ = x.shape
    widths = (SSM_W, MLA_Q_RANK, MLA_KV_RANK, 3 * HY_W, MLA_ROPE)
    row = lambda w: pl.BlockSpec((None, ROW_TILE, w), lambda b, i: (b, i, 0))
    return pl.pallas_call(
        _inproj_kernel,
        grid=(bsz, t // ROW_TILE),
        in_specs=[
            row(D_MODEL),
            _mod_spec(layer, n_ctx_tiles),
            pl.BlockSpec((None, 1, D_MODEL), lambda b, i: (layer, 0, 0)),
            pl.BlockSpec((None, D_MODEL, _IN_USED), lambda b, i: (layer, 0, 0)),
        ],
        out_specs=[row(w) for w in widths],
        out_shape=[jax.ShapeDtypeStruct((bsz, t, w), _BF16) for w in widths],
        compiler_params=_cparams("parallel", "parallel"),
        name="input_projection",
    )(x, mod_tab, norm1, w_in_used)


def _max4(a, b, c, d):
    hi1, lo1 = jnp.maximum(a, b), jnp.minimum(a, b)
    hi2, lo2 = jnp.maximum(c, d), jnp.minimum(c, d)
    return jnp.maximum(hi1, hi2), jnp.maximum(jnp.minimum(hi1, hi2), jnp.maximum(lo1, lo2))


def _first_argmax(vals):
    best, idx = vals[0], jnp.zeros(vals[0].shape, jnp.int32)
    for j in range(1, len(vals)):
        upd = vals[j] > best
        idx = jnp.where(upd, j, idx)
        best = jnp.where(upd, vals[j], best)
    return idx


def _pick(vals, idx):
    out = vals[-1]
    for j in range(len(vals) - 2, -1, -1):
        out = jnp.where(idx == j, vals[j], out)
    return out


def _route_rows(logits_t, bias_col):
    s = jax.nn.sigmoid(logits_t)
    sel = s + bias_col
    s_rows = [s[e:e + 1, :] for e in range(N_EXPERTS)]
    sel_rows = [sel[e:e + 1, :] for e in range(N_EXPERTS)]
    gsum = []
    for g in range(N_EXPERT_GROUPS):
        top, second = _max4(*sel_rows[EXPERTS_PER_GROUP * g:EXPERTS_PER_GROUP * (g + 1)])
        gsum.append(top + second)
    gidx = _first_argmax(gsum)
    v = [_pick([sel_rows[EXPERTS_PER_GROUP * g + j] for g in range(N_EXPERT_GROUPS)], gidx)
         for j in range(EXPERTS_PER_GROUP)]
    u = [_pick([s_rows[EXPERTS_PER_GROUP * g + j] for g in range(N_EXPERT_GROUPS)], gidx)
         for j in range(EXPERTS_PER_GROUP)]
    i1 = _first_argmax(v)
    neg = jnp.full(v[0].shape, -jnp.inf, _F32)
    i2 = _first_argmax([jnp.where(i1 == j, neg, v[j]) for j in range(EXPERTS_PER_GROUP)])
    wa, wb = _pick(u, i1), _pick(u, i2)
    tot = wa + wb
    return EXPERTS_PER_GROUP * gidx + i1, EXPERTS_PER_GROUP * gidx + i2, wa / tot, wb / tot


def _expert_ranks(e0, e1, tri_ref, run_ref):
    n = e0.shape[1]
    rows = lax.broadcasted_iota(jnp.int32, (N_EXPERTS, n), 0)
    oh0, oh1 = rows == e0, rows == e1
    c = jnp.where(oh0, 1.0, 0.0) + jnp.where(oh1, 1.0, 0.0)
    before = _dot(c.astype(_BF16), tri_ref[...])
    run = run_ref[...]
    base = before + jnp.concatenate([run] * (n // V7X_LANES), 1)
    rank0 = jnp.sum(jnp.where(oh0, base, 0.0), 0, keepdims=True)
    rank1 = jnp.sum(jnp.where(oh1, base, 0.0), 0, keepdims=True)
    run_ref[...] = run + jnp.broadcast_to(jnp.sum(c, 1, keepdims=True), run.shape)
    return rank0, rank1


def _merge_kernel(x_ref, ys_ref, ya_ref, yhc_ref, yhl_ref, mod_ref, g1_ref, g2_ref, wg_ref, wb_ref, wo_ref,
                  rwh_ref, rwl_ref, rb_ref, tri_ref, xo_ref, hl_ref, rt_ref, cnt_ref, run_ref, *, n_ctx_tiles):
    x = x_ref[...]
    xn = _modulated_norm(x, g1_ref[...], mod_ref[0:1, :], mod_ref[1:2, :]).astype(_BF16)
    y_hy = jnp.where(pl.program_id(1) < n_ctx_tiles, yhc_ref[...], yhl_ref[...])
    m = None
    for k, y in enumerate((ys_ref[...], ya_ref[...], y_hy)):
        gate = jax.nn.sigmoid(_dot(xn, wg_ref[:, k * D_MODEL:(k + 1) * D_MODEL]))
        term = gate * _dot(y, wb_ref[k])
        m = term if m is None else m + term
    x_new = x + mod_ref[2:3, :] * _dot(m.astype(_BF16), wo_ref[...])
    xo_ref[...] = x_new
    hl = _modulated_norm(x_new, g2_ref[...], mod_ref[3:4, :], mod_ref[4:5, :])
    hl_ref[...] = hl
    hl_hi = hl.astype(_BF16)
    hl_lo = (hl - hl_hi.astype(_F32)).astype(_BF16)
    nt = (((1,), (1,)), ((), ()))
    logits_t = (lax.dot_general(rwh_ref[...], hl_hi, nt, preferred_element_type=_F32)
                + lax.dot_general(rwl_ref[...], hl_hi, nt, preferred_element_type=_F32)
                + lax.dot_general(rwh_ref[...], hl_lo, nt, preferred_element_type=_F32))
    e0, e1, w0, w1 = _route_rows(logits_t, rb_ref[...])

    @pl.when((pl.program_id(0) == 0) & (pl.program_id(1) == 0))
    def _():
        run_ref[...] = jnp.zeros_like(run_ref)

    rank0, rank1 = _expert_ranks(e0, e1, tri_ref, run_ref)
    cnt_ref[...] = run_ref[...]
    zero = jnp.zeros_like(w0)
    rt_ref[...] = jnp.concatenate([e0.astype(_F32), e1.astype(_F32), w0, w1, rank0, rank1, zero, zero], 0)


def merge_and_route(x, y_ssm, y_att, hy_ctx, hy_lat, mod_tab, norm1, norm2, w_gates, w_branch, w_out,
                    rw_hi, rw_lo, rb_col, layer, n_ctx_tiles):
    bsz, t, _ = x.shape
    row = lambda w: pl.BlockSpec((None, ROW_TILE, w), lambda b, i: (b, i, 0))
    whole = lambda *shape: pl.BlockSpec(shape, lambda b, i: (0,) * len(shape))
    return pl.pallas_call(
        functools.partial(_merge_kernel, n_ctx_tiles=n_ctx_tiles),
        grid=(bsz, t // ROW_TILE),
        in_specs=[
            row(D_MODEL), row(SSM_W), row(MLA_HEADS * MLA_V),
            pl.BlockSpec((None, ROW_TILE, HY_W), lambda b, i: (b, jnp.minimum(i, n_ctx_tiles - 1), 0)),
            pl.BlockSpec((None, ROW_TILE, HY_W), lambda b, i: (b, jnp.maximum(i - n_ctx_tiles, 0), 0)),
            _mod_spec(layer, n_ctx_tiles),
            pl.BlockSpec((None, 1, D_MODEL), lambda b, i: (layer, 0, 0)),
            pl.BlockSpec((None, 1, D_MODEL), lambda b, i: (layer, 0, 0)),
            pl.BlockSpec((None, D_MODEL, N_BRANCH * D_MODEL), lambda b, i: (layer, 0, 0)),
            pl.BlockSpec((None, N_BRANCH, SSM_W, D_MODEL), lambda b, i: (layer, 0, 0, 0)),
            pl.BlockSpec((None, D_MODEL, D_MODEL), lambda b, i: (layer, 0, 0)),
            whole(N_EXPERTS, D_MODEL), whole(N_EXPERTS, D_MODEL), whole(N_EXPERTS, 1),
            whole(ROW_TILE, ROW_TILE),
        ],
        out_specs=[row(D_MODEL), row(D_MODEL),
                   pl.BlockSpec((None, V7X_SUBLANES, ROW_TILE), lambda b, i: (b, 0, i)),
                   whole(N_EXPERTS, V7X_LANES)],
        out_shape=[jax.ShapeDtypeStruct((bsz, t, D_MODEL), _F32),
                   jax.ShapeDtypeStruct((bsz, t, D_MODEL), _F32),
                   jax.ShapeDtypeStruct((bsz, V7X_SUBLANES, t), _F32),
                   jax.ShapeDtypeStruct((N_EXPERTS, V7X_LANES), _F32)],
        scratch_shapes=[pltpu.VMEM((N_EXPERTS, V7X_LANES), _F32)],
        compiler_params=_cparams("arbitrary", "arbitrary"),
        name="merge_and_route",
    )(x, y_ssm, y_att, hy_ctx, hy_lat, mod_tab, norm1, norm2, w_gates, w_branch, w_out, rw_hi, rw_lo, rb_col,
      jnp.triu(jnp.ones((ROW_TILE, ROW_TILE), _BF16), 1))


def _moe_kernel(te_ref, tv_ref, xs_ref, wg_ref, wu_ref, wd_ref, o_ref):
    del te_ref

    @pl.when(tv_ref[pl.program_id(0)] > 0)
    def _():
        h = xs_ref[...].astype(_BF16)
        act = (_silu(_dot(h, wg_ref[...])) * _dot(h, wu_ref[...])).astype(_BF16)
        o_ref[...] = _dot(act, wd_ref[...]).astype(o_ref.dtype)

    @pl.when(tv_ref[pl.program_id(0)] == 0)
    def _():
        o_ref[...] = jnp.zeros_like(o_ref)


def grouped_experts(xs, tile_expert, tile_valid, w_gate, w_up, w_down, layer):
    rows = xs.shape[0]
    n_tiles = rows // MOE_TILE
    wspec = lambda a, b: pl.BlockSpec((None, None, a, b), lambda i, te, tv: (layer, te[i], 0, 0))
    return pl.pallas_call(
        _moe_kernel,
        grid_spec=pltpu.PrefetchScalarGridSpec(
            num_scalar_prefetch=2,
            grid=(n_tiles,),
            in_specs=[
                pl.BlockSpec((MOE_TILE, D_MODEL), lambda i, te, tv: (i, 0)),
                wspec(D_MODEL, D_EXPERT), wspec(D_MODEL, D_EXPERT), wspec(D_EXPERT, D_MODEL),
            ],
            out_specs=pl.BlockSpec((MOE_TILE, D_MODEL), lambda i, te, tv: (i, 0)),
        ),
        out_shape=jax.ShapeDtypeStruct((rows, D_MODEL), _BF16),
        compiler_params=_cparams("arbitrary"),
        name="grouped_experts",
    )(tile_expert, tile_valid, xs, w_gate, w_up, w_down)


def expert_dispatch_plan(route, counts):
    bsz, _, t = route.shape
    n = bsz * t
    slots = lambda r0: jnp.stack([route[:, r0, :], route[:, r0 + 1, :]], -1).reshape(n * 2).astype(jnp.int32)
    e, rank = slots(0), slots(4)
    count = counts[:, 0].astype(jnp.int32)
    tiles = (count + MOE_TILE - 1) // MOE_TILE
    tile_start = jnp.cumsum(tiles) - tiles
    dest = tile_start[e] * MOE_TILE + rank
    rows = 2 * n + N_EXPERTS * MOE_TILE
    n_tiles = rows // MOE_TILE
    tile_ids = jnp.arange(n_tiles, dtype=jnp.int32)
    tile_expert = jnp.clip(jnp.searchsorted(jnp.cumsum(tiles), tile_ids, side="right"), 0, N_EXPERTS - 1)
    tile_valid = (tile_ids < jnp.sum(tiles)).astype(jnp.int32)
    src = jnp.zeros((rows,), jnp.int32).at[dest].set(jnp.arange(2 * n, dtype=jnp.int32) // 2)
    dest2 = dest.reshape(n, 2)
    return src, tile_expert.astype(jnp.int32), tile_valid, dest2[:, 0], dest2[:, 1]


def _combine_kernel(x_ref, a_ref, b_ref, wa_ref, wb_ref, mod_ref, o_ref):
    wa, wb = wa_ref[...], wb_ref[...]
    for j in range(D_MODEL // V7X_LANES):
        sl = slice(j * V7X_LANES, (j + 1) * V7X_LANES)
        y = wa * a_ref[:, sl].astype(_F32) + wb * b_ref[:, sl].astype(_F32)
        o_ref[:, sl] = x_ref[:, sl] + mod_ref[5:6, sl] * y


def combine_experts(x, ya, yb, wa, wb, mod_tab, layer, n_ctx_tiles):
    bsz, t, _ = x.shape
    row = lambda w: pl.BlockSpec((None, ROW_TILE, w), lambda b, i: (b, i, 0))
    return pl.pallas_call(
        _combine_kernel,
        grid=(bsz, t // ROW_TILE),
        in_specs=[row(D_MODEL), row(D_MODEL), row(D_MODEL), row(V7X_LANES), row(V7X_LANES),
                  _mod_spec(layer, n_ctx_tiles)],
        out_specs=row(D_MODEL),
        out_shape=jax.ShapeDtypeStruct(x.shape, _F32),
        compiler_params=_cparams("parallel", "parallel"),
        name="combine_experts",
    )(x, ya, yb, wa, wb, mod_tab)


def moe_block(x_new, hl, route, counts, mod_tab, w_gate, w_up, w_down, layer, n_ctx_tiles):
    bsz, t, _ = x_new.shape
    src, tile_expert, tile_valid, d0, d1 = expert_dispatch_plan(route, counts)
    xs = jnp.take(hl.reshape(bsz * t, D_MODEL), src, axis=0, mode="clip")
    ys = grouped_experts(xs, tile_expert, tile_valid, w_gate, w_up, w_down, layer)
    ya = jnp.take(ys, d0, axis=0, mode="clip").reshape(bsz, t, D_MODEL)
    yb = jnp.take(ys, d1, axis=0, mode="clip").reshape(bsz, t, D_MODEL)
    lanes = lambda r: jnp.broadcast_to(route[:, r, :, None], (bsz, t, V7X_LANES))
    return combine_experts(x_new, ya, yb, lanes(2), lanes(3), mod_tab, layer, n_ctx_tiles)


ATT_TILE = 256
ATT_CHUNK_TILES = 4
ATT_UNROLL = 8
V_ROWS = MLA_V + 16
Q_SCALE = MLA_QK ** -0.5 * math.log2(math.e)
_ROPE_LO = MLA_NOPE
_ROPE_HALF = MLA_ROPE // 2


def rope_tables(n_ctx, n_lat):
    rows = n_lat // GRID_W
    row = jnp.broadcast_to(jnp.arange(rows, dtype=_F32)[:, None], (rows, GRID_W)).reshape(n_lat)
    col = jnp.broadcast_to(jnp.arange(GRID_W, dtype=_F32)[None, :], (rows, GRID_W)).reshape(n_lat)
    n_f = MLA_ROPE // 4
    inv = ROPE_THETA ** (-jnp.arange(n_f, dtype=_F32) / n_f)
    ang = jnp.concatenate([row[:, None] * inv, col[:, None] * inv], -1)
    cos = jnp.concatenate([jnp.ones((n_ctx, _ROPE_HALF), _F32), jnp.cos(ang)], 0)
    sin = jnp.concatenate([jnp.zeros((n_ctx, _ROPE_HALF), _F32), jnp.sin(ang)], 0)
    t = n_ctx + n_lat
    ones, zeros = jnp.ones((t, MLA_NOPE), _F32), jnp.zeros((t, MLA_NOPE), _F32)
    tail1, tail0 = jnp.ones((t, HEAD_PAD - MLA_QK), _F32), jnp.zeros((t, HEAD_PAD - MLA_QK), _F32)
    cos_p = jnp.concatenate([ones, cos, cos, tail1], -1)
    sin_p = jnp.concatenate([zeros, -sin, sin, tail0], -1)
    return cos.T, sin.T, cos_p, sin_p


def pack_mla_weights(w_uq, w_ukv, qk_norm_q, qk_norm_k):
    depth = w_uq.shape[0]
    pad = HEAD_PAD - MLA_QK
    wq = w_uq.reshape(depth, MLA_Q_RANK, MLA_HEADS, MLA_QK)
    wq = jnp.pad(wq, ((0, 0), (0, 0), (0, 0), (0, pad))).reshape(depth, MLA_Q_RANK, MLA_HEADS * HEAD_PAD)
    wq_t = jnp.swapaxes(wq, 1, 2).astype(_BF16)
    wkv = w_ukv.reshape(depth, MLA_KV_RANK, MLA_HEADS, MLA_NOPE + MLA_V)
    wk = jnp.pad(wkv[..., :MLA_NOPE], ((0, 0), (0, 0), (0, 0), (0, HEAD_PAD - MLA_NOPE)))
    wk = wk.reshape(depth, MLA_KV_RANK, MLA_HEADS * HEAD_PAD).astype(_BF16)
    wv_t = jnp.swapaxes(wkv[..., MLA_NOPE:].reshape(depth, MLA_KV_RANK, MLA_HEADS * MLA_V), 1, 2).astype(_BF16)
    place = jnp.zeros((MLA_ROPE, HEAD_PAD), _F32).at[jnp.arange(MLA_ROPE), _ROPE_LO + jnp.arange(MLA_ROPE)].set(1.0)
    place = jnp.tile(place, (1, MLA_HEADS)).astype(_BF16)
    gq = jnp.pad(qk_norm_q, ((0, 0), (0, pad)))
    gq_col = jnp.broadcast_to(gq[:, :, None], (depth, HEAD_PAD, ATT_TILE)).astype(_F32)
    gk_row = jnp.pad(qk_norm_k, ((0, 0), (0, pad))).reshape(depth, 1, HEAD_PAD).astype(_F32)
    return wq_t, wk, wv_t, place, gq_col, gk_row


def _mla_prep_kernel(ql_ref, kvl_ref, kr_ref, qn_ref, kvn_ref, wq_ref, wk_ref, wv_ref, place_ref, gq_ref, gk_ref,
                     cos_t_ref, sin_t_ref, cos_p_ref, sin_p_ref, qt_ref, k_ref, vt_ref):
    nt = (((1,), (1,)), ((), ()))
    ql = ql_ref[...].astype(_F32)
    qn = (ql * lax.rsqrt(jnp.mean(ql * ql, -1, keepdims=True) + EPS) * qn_ref[...]).astype(_BF16)
    kvl = kvl_ref[...].astype(_F32)
    kvn = (kvl * lax.rsqrt(jnp.mean(kvl * kvl, -1, keepdims=True) + EPS) * kvn_ref[...]).astype(_BF16)

    q_t = lax.dot_general(wq_ref[...], qn, nt, preferred_element_type=_F32)
    cos_t, sin_t = cos_t_ref[...], sin_t_ref[...]
    scale = Q_SCALE
    lo, mid, hi = _ROPE_LO, _ROPE_LO + _ROPE_HALF, _ROPE_LO + MLA_ROPE
    for h in range(MLA_HEADS):
        blk = q_t[h * HEAD_PAD:(h + 1) * HEAD_PAD, :]
        ms = jnp.sum(blk * blk, 0, keepdims=True) * (1.0 / MLA_QK)
        y = blk * (lax.rsqrt(ms + EPS) * scale) * gq_ref[...]
        r1, r2 = y[lo:mid, :], y[mid:hi, :]
        rot = jnp.concatenate([y[:lo, :], r1 * cos_t - r2 * sin_t, r2 * cos_t + r1 * sin_t, y[hi:, :]], 0)
        qt_ref[h] = rot.astype(qt_ref.dtype)

    k_pre = _dot(kvn, wk_ref[...]) + _dot(kr_ref[...], place_ref[...])
    cos_p, sin_p = cos_p_ref[...], sin_p_ref[...]
    lane = lax.broadcasted_iota(jnp.int32, cos_p.shape, 1)
    for h in range(MLA_HEADS):
        blk = k_pre[:, h * HEAD_PAD:(h + 1) * HEAD_PAD]
        ms = jnp.sum(blk * blk, -1, keepdims=True) * (1.0 / MLA_QK)
        y = blk * lax.rsqrt(ms + EPS) * gk_ref[...]
        swap = jnp.where(lane < mid, pltpu.roll(y, HEAD_PAD - _ROPE_HALF, 1), pltpu.roll(y, _ROPE_HALF, 1))
        k_ref[h] = (y * cos_p + swap * sin_p).astype(k_ref.dtype)

    v_t = lax.dot_general(wv_ref[...], kvn, nt, preferred_element_type=_F32)
    ones = jnp.ones((V_ROWS - MLA_V, v_t.shape[1]), _F32)
    for h in range(MLA_HEADS):
        vt_ref[h] = jnp.concatenate([v_t[h * MLA_V:(h + 1) * MLA_V, :], ones], 0).astype(vt_ref.dtype)


def mla_prepare(q_lat, kv_lat, k_rope, q_norm, kv_norm, packed, tables, layer):
    bsz, t, _ = q_lat.shape
    wq_t, wk, wv_t, place, gq_col, gk_row = packed
    cos_t, sin_t, cos_p, sin_p = tables
    n_tiles = t // ATT_TILE
    row = lambda w: pl.BlockSpec((None, ATT_TILE, w), lambda b, i: (b, i, 0))
    lay = lambda *shape: pl.BlockSpec((None,) + shape, lambda b, i: (layer,) + (0,) * len(shape))
    return pl.pallas_call(
        _mla_prep_kernel,
        grid=(bsz, n_tiles),
        in_specs=[
            row(MLA_Q_RANK), row(MLA_KV_RANK), row(MLA_ROPE),
            lay(1, MLA_Q_RANK), lay(1, MLA_KV_RANK),
            lay(MLA_HEADS * HEAD_PAD, MLA_Q_RANK), lay(MLA_KV_RANK, MLA_HEADS * HEAD_PAD),
            lay(MLA_HEADS * MLA_V, MLA_KV_RANK),
            pl.BlockSpec((MLA_ROPE, MLA_HEADS * HEAD_PAD), lambda b, i: (0, 0)),
            lay(HEAD_PAD, ATT_TILE), lay(1, HEAD_PAD),
            pl.BlockSpec((_ROPE_HALF, ATT_TILE), lambda b, i: (0, i)),
            pl.BlockSpec((_ROPE_HALF, ATT_TILE), lambda b, i: (0, i)),
            pl.BlockSpec((ATT_TILE, HEAD_PAD), lambda b, i: (i, 0)),
            pl.BlockSpec((ATT_TILE, HEAD_PAD), lambda b, i: (i, 0)),
        ],
        out_specs=[
            pl.BlockSpec((None, MLA_HEADS, HEAD_PAD, ATT_TILE), lambda b, i: (b, 0, 0, i)),
            pl.BlockSpec((None, MLA_HEADS, ATT_TILE, HEAD_PAD), lambda b, i: (b, 0, i, 0)),
            pl.BlockSpec((None, MLA_HEADS, None, V_ROWS, ATT_TILE), lambda b, i: (b, 0, i, 0, 0)),
        ],
        out_shape=[
            jax.ShapeDtypeStruct((bsz, MLA_HEADS, HEAD_PAD, t), _BF16),
            jax.ShapeDtypeStruct((bsz, MLA_HEADS, t, HEAD_PAD), _BF16),
            jax.ShapeDtypeStruct((bsz, MLA_HEADS, n_tiles, V_ROWS, ATT_TILE), _BF16),
        ],
        compiler_params=_cparams("parallel", "parallel"),
        name="mla_prepare",
    )(q_lat, kv_lat, k_rope, q_norm, kv_norm, wq_t, wk, wv_t, place, gq_col, gk_row, cos_t, sin_t, cos_p, sin_p)


def _attention_scores(s_ref, slot, q_t, k_ref, tile0, n_sub):
    for j in range(n_sub):
        row0 = (tile0 + j) * ATT_TILE
        if not isinstance(row0, int):
            row0 = pl.multiple_of(row0, ATT_TILE)
        s_ref[slot, j * ATT_TILE:(j + 1) * ATT_TILE, :] = _dot(k_ref[pl.ds(row0, ATT_TILE), :], q_t)


def _attention_update(s_ref, slot, vt_ref, tile0, n_sub, carry):
    m, acc = carry
    tiles = [s_ref[slot, j * ATT_TILE:(j + 1) * ATT_TILE, :] for j in range(n_sub)]
    m_new = m
    for s_j in tiles:
        m_new = jnp.maximum(m_new, jnp.max(s_j, 0, keepdims=True))
    acc = jnp.exp2(m - m_new) * acc
    for j, s_j in enumerate(tiles):
        p = jnp.exp2((s_j - m_new).astype(_BF16))
        acc = acc + _dot(vt_ref[tile0 + j], p)
    return m_new, acc


def _attention_kernel(qt_ref, k_ref, vt_ref, o_ref, s_ref, *, n_ctx_tiles, n_tiles, n_sub, unroll):
    i = pl.program_id(2)
    q_t = qt_ref[...]
    tq = q_t.shape[1]
    n_chunks = (n_tiles - n_ctx_tiles) // n_sub
    first = lambda c: n_ctx_tiles + c * n_sub
    carry = (jnp.full((1, tq), -jnp.inf, _F32), jnp.zeros((V_ROWS, tq), _F32))
    _attention_scores(s_ref, 1, q_t, k_ref, 0, n_ctx_tiles)
    carry = _attention_update(s_ref, 1, vt_ref, 0, n_ctx_tiles, carry)

    def group(base, cr):
        _attention_scores(s_ref, 0, q_t, k_ref, first(base), n_sub)
        for c in range(unroll):
            if c + 1 < unroll:
                _attention_scores(s_ref, (c + 1) % 2, q_t, k_ref, first(base + c + 1), n_sub)
            cr = _attention_update(s_ref, c % 2, vt_ref, first(base + c), n_sub, cr)
        return cr

    def latent_keys(cr):
        if n_chunks == unroll:
            return group(0, cr)
        return lax.fori_loop(0, n_chunks // unroll, lambda g, c: group(g * unroll, c), cr)

    _, acc = lax.cond(i >= n_ctx_tiles, latent_keys, lambda cr: cr, carry)
    o_ref[...] = (acc[:MLA_V] / acc[MLA_V:MLA_V + 1]).astype(o_ref.dtype)


def flash_attention(q_t, k, v_t, n_ctx_tiles):
    bsz, heads, _, t = q_t.shape
    n_tiles = t // ATT_TILE
    n_lat_tiles = n_tiles - n_ctx_tiles
    n_sub = math.gcd(n_lat_tiles, ATT_CHUNK_TILES)
    unroll = math.gcd(n_lat_tiles // n_sub, ATT_UNROLL)
    return pl.pallas_call(
        functools.partial(_attention_kernel, n_ctx_tiles=n_ctx_tiles, n_tiles=n_tiles, n_sub=n_sub, unroll=unroll),
        grid=(bsz, heads, n_tiles),
        in_specs=[
            pl.BlockSpec((None, None, HEAD_PAD, ATT_TILE), lambda b, h, i: (b, h, 0, i)),
            pl.BlockSpec((None, None, t, HEAD_PAD), lambda b, h, i: (b, h, 0, 0)),
            pl.BlockSpec((None, None, n_tiles, V_ROWS, ATT_TILE), lambda b, h, i: (b, h, 0, 0, 0)),
        ],
        out_specs=pl.BlockSpec((None, MLA_V, ATT_TILE), lambda b, h, i: (b, h, i)),
        out_shape=jax.ShapeDtypeStruct((bsz, heads * MLA_V, t), _BF16),
        scratch_shapes=[pltpu.VMEM((2, max(n_sub, n_ctx_tiles) * ATT_TILE, ATT_TILE), _F32)],
        compiler_params=_cparams("parallel", "parallel", "arbitrary"),
        name="flash_attention",
    )(q_t, k, v_t)


S5_KG = SSM_W // V7X_LANES
S5_GPK = SSM_GROUPS // S5_KG
S5_HALF = S5_GPK * SSM_STATE
S5_COLS = 2 * SSM_GROUPS * SSM_STATE
S5_PAIRS = SSM_GROUPS * SSM_STATE // V7X_LANES
S5_LEVELS = (1, 2, 4)


def _s5_disc_kernel(lr_ref, li_ref, ldt_ref, br_ref, bi_ref, pr_ref, pi_ref, bbr_ref, bbi_ref):
    lr = jnp.minimum(lr_ref[...], -1e-4)
    li = li_ref[...]
    dt = jnp.exp(ldt_ref[...])
    mag = jnp.exp(lr * dt)
    ar, ai = mag * jnp.cos(li * dt), mag * jnp.sin(li * dt)
    den = lr * lr + li * li
    gr = ((ar - 1.0) * lr + ai * li) / den
    gi = (ai * lr - (ar - 1.0) * li) / den
    br, bi = br_ref[...], bi_ref[...]
    bbr_ref[...] = gr * br - gi * bi
    bbi_ref[...] = gr * bi + gi * br
    pr, pi = ar, ai
    for k in range(V7X_SUBLANES):
        pr_ref[k] = pr
        pi_ref[k] = pi
        pr, pi = pr * ar - pi * ai, pr * ai + pi * ar


def s5_discretize(lam_re, lam_im, log_dt, b_re, b_im):
    depth = lam_re.shape[0]
    ld = depth * 2
    g, p, j = SSM_GROUPS, SSM_STATE, SSM_GROUP
    flat = lambda a: a.reshape((ld,) + a.shape[2:])
    g1p = lambda a: a.reshape(ld, g, 1, p)
    ldt = jnp.broadcast_to(flat(log_dt)[:, :, None, None], (ld, g, 1, p))
    b_t = lambda a: jnp.swapaxes(flat(a), 2, 3)
    gp = pl.BlockSpec((None, g, 1, p), lambda i: (i, 0, 0, 0))
    gjp = pl.BlockSpec((None, g, j, p), lambda i: (i, 0, 0, 0))
    pw = pl.BlockSpec((None, V7X_SUBLANES, g, 1, p), lambda i: (i, 0, 0, 0, 0))
    pow_r, pow_i, bbr, bbi = pl.pallas_call(
        _s5_disc_kernel,
        grid=(ld,),
        in_specs=[gp, gp, gp, gjp, gjp],
        out_specs=[pw, pw, gjp, gjp],
        out_shape=[jax.ShapeDtypeStruct((ld, V7X_SUBLANES, g, 1, p), _F32)] * 2
        + [jax.ShapeDtypeStruct((ld, g, j, p), _F32)] * 2,
        compiler_params=_cparams("parallel"),
        name="s5_discretize",
    )(g1p(lam_re), g1p(lam_im), ldt, b_t(b_re), b_t(b_im))
    return pow_r.reshape(ld, V7X_SUBLANES, g, p), pow_i.reshape(ld, V7X_SUBLANES, g, p), bbr, bbi


def pack_s5(pow_r, pow_i, bbr, bbi, c_re, c_im):
    ld = pow_r.shape[0]
    n = SSM_GROUPS * SSM_STATE
    rev = (jnp.arange(ld) % 2 == 1)[:, None, None]
    t = jnp.arange(V7X_SUBLANES)[None, :, None]
    pr, pi = pow_r.reshape(ld, V7X_SUBLANES, n), pow_i.reshape(ld, V7X_SUBLANES, n)
    consts = []
    for s in S5_LEVELS:
        keep = jnp.where(rev, t < V7X_SUBLANES - s, t >= s)
        for a in (pr, pi):
            consts.append(jnp.where(keep, a[:, s - 1:s, :], 0.0))
    for a in (pr, pi):
        consts.append(jnp.where(rev, a[:, ::-1, :], a))
    consts = jnp.stack(consts, 1)
    eye = jnp.eye(S5_GPK, dtype=_F32)
    bb = jnp.stack([bbr, bbi], 1).reshape(ld, 2, S5_KG, S5_GPK, SSM_GROUP, SSM_STATE)
    bd = jnp.einsum("dckgjp,gh->dkgjchp", bb, eye).reshape(ld, S5_KG, V7X_LANES, 2 * S5_HALF).astype(_BF16)
    cc = jnp.stack([c_re, -c_im], 2)
    cc = cc.reshape(ld, 2, S5_KG, S5_GPK, SSM_GROUP, SSM_STATE)
    cd = jnp.einsum("dckgjp,gh->dkcgphj", cc, eye).reshape(ld, S5_KG, 2 * S5_HALF, V7X_LANES).astype(_BF16)
    return consts, bd, cd


def _s5_scan_kernel(u_ref, bd_ref, cd_ref, k_ref, *rest, reverse, finish):
    if finish:
        yf_ref, d_ref, wg_ref, bg_ref, o_ref, bu_ref, carry_ref = rest
    else:
        o_ref, bu_ref, carry_ref = rest
    n_rows = u_ref.shape[0]

    @pl.when(pl.program_id(1) == 0)
    def _():
        carry_ref[...] = jnp.zeros_like(carry_ref)

    u = u_ref[...]
    for kg in range(S5_KG):
        bu_ref[:, kg * 2 * S5_HALF:(kg + 1) * 2 * S5_HALF] = _dot(u[:, kg * V7X_LANES:(kg + 1) * V7X_LANES], bd_ref[kg])

    n_vregs = n_rows // V7X_SUBLANES
    last = 0 if reverse else V7X_SUBLANES - 1

    def step(r, _):
        row = pl.multiple_of((n_vregs - 1 - r if reverse else r) * V7X_SUBLANES, V7X_SUBLANES)
        for jp in range(S5_PAIRS):
            kg, q = divmod(jp, S5_PAIRS // S5_KG)
            cr = kg * 2 * S5_HALF + q * V7X_LANES
            ci = cr + S5_HALF
            cl = jp * V7X_LANES
            xr = bu_ref[pl.ds(row, V7X_SUBLANES), cr:cr + V7X_LANES]
            xi = bu_ref[pl.ds(row, V7X_SUBLANES), ci:ci + V7X_LANES]
            for lvl, s in enumerate(S5_LEVELS):
                shift = V7X_SUBLANES - s if reverse else s
                sr, si = pltpu.roll(xr, shift, 0), pltpu.roll(xi, shift, 0)
                ar = k_ref[2 * lvl, :, cl:cl + V7X_LANES]
                ai = k_ref[2 * lvl + 1, :, cl:cl + V7X_LANES]
                xr, xi = xr + (ar * sr - ai * si), xi + (ar * si + ai * sr)
            pr = k_ref[2 * len(S5_LEVELS), :, cl:cl + V7X_LANES]
            pi = k_ref[2 * len(S5_LEVELS) + 1, :, cl:cl + V7X_LANES]
            hr0 = carry_ref[:, cr:cr + V7X_LANES]
            hi0 = carry_ref[:, ci:ci + V7X_LANES]
            xr, xi = xr + (pr * hr0 - pi * hi0), xi + (pr * hi0 + pi * hr0)
            bu_ref[pl.ds(row, V7X_SUBLANES), cr:cr + V7X_LANES] = xr
            bu_ref[pl.ds(row, V7X_SUBLANES), ci:ci + V7X_LANES] = xi
            carry_ref[:, cr:cr + V7X_LANES] = jnp.broadcast_to(xr[last:last + 1, :], xr.shape)
            carry_ref[:, ci:ci + V7X_LANES] = jnp.broadcast_to(xi[last:last + 1, :], xi.shape)
        return 0

    lax.fori_loop(0, n_vregs, step, 0)

    ys = [_dot(bu_ref[:, kg * 2 * S5_HALF:(kg + 1) * 2 * S5_HALF].astype(_BF16), cd_ref[kg]) for kg in range(S5_KG)]
    y = jnp.concatenate(ys, -1)
    if finish:
        y = y + yf_ref[...] + d_ref[...] * u.astype(_F32)
        z = jax.nn.gelu(y)
        o_ref[...] = (z * jax.nn.sigmoid(_dot(z.astype(_BF16), wg_ref[...]) + bg_ref[...])).astype(o_ref.dtype)
    else:
        o_ref[...] = y


def s5_scan(u, consts, bd, cd, layer, n_ctx_tiles, reverse, finish_args=None):
    bsz, t, _ = u.shape
    n_tiles = t // ROW_TILE
    d = 2 * layer + (1 if reverse else 0)
    if reverse:
        chunk = lambda i: jnp.where(i < n_ctx_tiles, n_ctx_tiles - 1 - i, n_tiles - 1 - (i - n_ctx_tiles))
    else:
        chunk = lambda i: i
    row = lambda w: pl.BlockSpec((None, ROW_TILE, w), lambda b, i: (b, chunk(i), 0))
    lay = lambda *shape: pl.BlockSpec((None,) + shape, lambda b, i: (d,) + (0,) * len(shape))
    in_specs = [row(SSM_W), lay(S5_KG, V7X_LANES, 2 * S5_HALF), lay(S5_KG, 2 * S5_HALF, V7X_LANES),
                lay(2 * len(S5_LEVELS) + 2, V7X_SUBLANES, SSM_GROUPS * SSM_STATE)]
    args = [u, bd, cd, consts]
    finish = finish_args is not None
    if finish:
        y_fwd, d_skip, w_glu, b_glu = finish_args
        lyr = lambda *shape: pl.BlockSpec((None,) + shape, lambda b, i: (layer,) + (0,) * len(shape))
        in_specs += [row(SSM_W), lyr(1, SSM_W), lyr(SSM_W, SSM_W), lyr(1, SSM_W)]
        args += [y_fwd, d_skip, w_glu, b_glu]
    return pl.pallas_call(
        functools.partial(_s5_scan_kernel, reverse=reverse, finish=finish),
        grid=(bsz, n_tiles),
        in_specs=in_specs,
        out_specs=row(SSM_W),
        out_shape=jax.ShapeDtypeStruct((bsz, t, SSM_W), _BF16 if finish else _F32),
        scratch_shapes=[
            pltpu.VMEM((ROW_TILE, S5_COLS), _F32),
            pltpu.VMEM((V7X_SUBLANES, S5_COLS), _F32),
        ],
        compiler_params=_cparams("parallel", "arbitrary"),
        name="s5_scan_bwd" if reverse else "s5_scan_fwd",
    )(*args)


HY_CT = 512
HY_FEAT_PAD = V7X_LANES
HY_TAP_COLS = HY_ORDER * HY_W


def _short_conv_kernel(prev_ref, u_ref, next_ref, w_ref, b_ref, *out_refs, n_ctx, n_total):
    i = pl.program_id(1)
    u = u_ref[...].astype(_F32)
    rows = u.shape[0]
    r = lax.broadcasted_iota(jnp.int32, u.shape, 0)
    g = r + i * rows
    before = jnp.broadcast_to(prev_ref[V7X_SUBLANES - 1:V7X_SUBLANES, :].astype(_F32), u.shape)
    after = jnp.broadcast_to(next_ref[0:1, :].astype(_F32), u.shape)
    up = jnp.where(r == 0, before, pltpu.roll(u, 1, 0))
    un = jnp.where(r == rows - 1, after, pltpu.roll(u, rows - 1, 0))
    up = jnp.where((g == 0) | (g == n_ctx), 0.0, up)
    un = jnp.where((g == n_ctx - 1) | (g == n_total - 1), 0.0, un)
    us = w_ref[0:1, :] * up + w_ref[1:2, :] * u + w_ref[2:3, :] * un + b_ref[...]
    parts = [us[:, k * HY_W:(k + 1) * HY_W] for k in range(3)]

    @pl.when(i * rows < n_ctx)
    def _():
        for ref, part in zip(out_refs[:3], parts):
            ref[...] = part.astype(ref.dtype)

    @pl.when(i * rows >= n_ctx)
    def _():
        for ref, part in zip(out_refs[3:], parts):
            ref[...] = part.astype(ref.dtype)


def hyena_short_conv(hy, conv_w, conv_b, layer, n_ctx):
    bsz, t, width = hy.shape
    per = ROW_TILE // V7X_SUBLANES
    n_halo = t // V7X_SUBLANES
    n_ctx_tiles = n_ctx // ROW_TILE
    row = lambda w: pl.BlockSpec((None, ROW_TILE, w), lambda b, i: (b, i, 0))
    return pl.pallas_call(
        functools.partial(_short_conv_kernel, n_ctx=n_ctx, n_total=t),
        grid=(bsz, t // ROW_TILE),
        in_specs=[
            pl.BlockSpec((None, V7X_SUBLANES, width), lambda b, i: (b, jnp.maximum(i * per - 1, 0), 0)),
            row(width),
            pl.BlockSpec((None, V7X_SUBLANES, width), lambda b, i: (b, jnp.minimum((i + 1) * per, n_halo - 1), 0)),
            pl.BlockSpec((None, conv_w.shape[1], width), lambda b, i: (layer, 0, 0)),
            pl.BlockSpec((None, 1, width), lambda b, i: (layer, 0, 0)),
        ],
        out_specs=[pl.BlockSpec((None, ROW_TILE, HY_W), lambda b, i: (b, jnp.minimum(i, n_ctx_tiles - 1), 0))] * 3
        + [pl.BlockSpec((None, ROW_TILE, HY_W), lambda b, i: (b, jnp.maximum(i - n_ctx_tiles, 0), 0))] * 3,
        out_shape=[jax.ShapeDtypeStruct((bsz, n_ctx, HY_W), _BF16)] * 3
        + [jax.ShapeDtypeStruct((bsz, t - n_ctx, HY_W), _BF16)] * 3,
        compiler_params=_cparams("arbitrary", "arbitrary"),
        name="hyena_short_conv",
    )(hy, hy, hy, conv_w, conv_b)


def hyena_features(n):
    m = jnp.arange(2 * n, dtype=jnp.int32)
    t = jnp.where(m < n, m, jnp.where(m == n, 0, 2 * n - m)).astype(_F32)
    t_norm = t / (n - 1)
    bands = jnp.linspace(1e-4, HY_BANDS - 1, HY_BANDS, dtype=_F32)
    ang = (2.0 * math.pi * t / n)[:, None] * bands[None, :]
    feat = jnp.concatenate([t_norm[:, None], jnp.cos(ang), -jnp.sin(ang)], -1)
    feat = jnp.pad(feat, ((0, 0), (0, HY_FEAT_PAD - HY_EMB)))
    return feat, jnp.broadcast_to(t_norm[:, None], (2 * n, V7X_LANES))


def _filter_kernel(feat_ref, tn_ref, w1_ref, b1_ref, w2_ref, b2_ref, w3_ref, fr_ref, dec_ref, h_ref, l1_ref, *, n):
    i = pl.program_id(0)
    hp = lax.Precision.HIGHEST
    fr = fr_ref[...]
    h = jnp.sin(fr * (jnp.dot(feat_ref[...], w1_ref[...], preferred_element_type=_F32, precision=hp) + b1_ref[...]))
    h = jnp.sin(fr * (jnp.dot(h, w2_ref[...], preferred_element_type=_F32, precision=hp) + b2_ref[...]))
    h = _dot(h.astype(_BF16), w3_ref[...])
    tn = tn_ref[...]
    rows = h.shape[0]
    zero_tap = (lax.broadcasted_iota(jnp.int32, (rows, V7X_LANES), 0) + i * rows) == n

    @pl.when(i == 0)
    def _():
        l1_ref[...] = jnp.zeros_like(l1_ref)

    for c in range(HY_TAP_COLS // V7X_LANES):
        sl = slice(c * V7X_LANES, (c + 1) * V7X_LANES)
        blk = jnp.where(zero_tap, 0.0, h[:, sl] * jnp.exp(-tn * jnp.abs(dec_ref[:, sl])))
        h_ref[:, sl] = blk.astype(h_ref.dtype)
        l1_ref[:, sl] += jnp.sum(jnp.abs(blk).reshape(rows // V7X_SUBLANES, V7X_SUBLANES, V7X_LANES), 0)


def hyena_filter_taps(n, w1, b1, w2, b2, w3_dir, freq, decay_dir, layer):
    feat, tn = hyena_features(n)
    rows = min(ROW_TILE, n)
    half = n // rows
    lay = lambda *shape: pl.BlockSpec((None,) + shape, lambda i: (layer,) + (0,) * len(shape))
    by_dir = lambda *shape: pl.BlockSpec((None, None) + shape,
                                         lambda i: (layer, jnp.where(i >= half, 1, 0)) + (0,) * len(shape))
    return pl.pallas_call(
        functools.partial(_filter_kernel, n=n),
        grid=(2 * half,),
        in_specs=[
            pl.BlockSpec((rows, HY_FEAT_PAD), lambda i: (i, 0)),
            pl.BlockSpec((rows, V7X_LANES), lambda i: (i, 0)),
            lay(HY_FEAT_PAD, HY_HIDDEN), lay(1, HY_HIDDEN), lay(HY_HIDDEN, HY_HIDDEN), lay(1, HY_HIDDEN),
            by_dir(HY_HIDDEN, HY_TAP_COLS), lay(1, HY_HIDDEN), by_dir(1, HY_TAP_COLS),
        ],
        out_specs=[pl.BlockSpec((rows, HY_TAP_COLS), lambda i: (i, 0)),
                   pl.BlockSpec((V7X_SUBLANES, HY_TAP_COLS), lambda i: (0, 0))],
        out_shape=[jax.ShapeDtypeStruct((2 * n, HY_TAP_COLS), _BF16),
                   jax.ShapeDtypeStruct((V7X_SUBLANES, HY_TAP_COLS), _F32)],
        compiler_params=_cparams("arbitrary"),
        name="hyena_filter_taps",
    )(feat, tn, w1, b1, w2, b2, w3_dir, freq, decay_dir)


def _dft_block(n_out, n_in, sign, scale=1.0, real_input=False):
    size = max(n_out, n_in)
    ang = 2.0 * np.pi * np.outer(np.arange(n_out), np.arange(n_in)) / size
    fr, fi = np.cos(ang) * scale, sign * np.sin(ang) * scale
    blk = np.concatenate([fr, fi], 0) if real_input else np.block([[fr, -fi], [fi, fr]])
    return jnp.asarray(blk, _F32).astype(_BF16)


def _twiddle_table(n1, n2, sign):
    ang = 2.0 * np.pi * np.outer(np.arange(n1), np.arange(n2)) / (n1 * n2)
    tw = np.stack([np.cos(ang), sign * np.sin(ang)], 0)[..., None]
    return jnp.broadcast_to(jnp.asarray(tw, _F32), (2, n1, n2, V7X_LANES))


def _cmul(ar, ai, br, bi):
    return ar * br - ai * bi, ar * bi + ai * br


def _lanes_mul(xr, xi, twr, twi):
    outs_r, outs_i = [], []
    for c in range(xr.shape[1] // V7X_LANES):
        sl = slice(c * V7X_LANES, (c + 1) * V7X_LANES)
        r, i = _cmul(xr[:, sl], xi[:, sl], twr, twi)
        outs_r.append(r)
        outs_i.append(i)
    return jnp.concatenate(outs_r, 1), jnp.concatenate(outs_i, 1)


def _dft_kernel(*refs, real_input, pre_tw, has_filter, has_second, post_tw, slabs):
    refs = list(refs)
    xr_ref = refs.pop(0)
    xi_ref = None if real_input else refs.pop(0)
    f_ref = refs.pop(0)
    tw_ref = refs.pop(0) if (pre_tw or post_tw) else None
    if has_filter:
        hr_ref, hi_ref, l1_ref = refs.pop(0), refs.pop(0), refs.pop(0)
    g_ref = refs.pop(0) if has_second else None
    or_ref, oi_ref = refs
    for a in range(slabs):
        xr = xr_ref[a].astype(_F32)
        if real_input:
            s = xr.astype(_BF16)
        else:
            xi = xi_ref[a].astype(_F32)
            if pre_tw:
                xr, xi = _lanes_mul(xr, xi, tw_ref[0, a], tw_ref[1, a])
            s = jnp.concatenate([xr, xi], 0).astype(_BF16)
        y = _dot(f_ref[...], s)
        m = y.shape[0] // 2
        yr, yi = y[:m], y[m:]
        if has_filter:
            inv = 1.0 / (jnp.sum(l1_ref[...], 0, keepdims=True) + EPS)
            yr, yi = _cmul(yr, yi, hr_ref[a].astype(_F32) * inv, hi_ref[a].astype(_F32) * inv)
        if has_second:
            z = _dot(g_ref[...], jnp.concatenate([yr, yi], 0).astype(_BF16))
            m = z.shape[0] // 2
            yr, yi = z[:m], z[m:]
        if post_tw:
            yr, yi = _lanes_mul(yr, yi, tw_ref[0, a], -tw_ref[1, a])
        or_ref[a] = yr.astype(or_ref.dtype)
        oi_ref[a] = yi.astype(oi_ref.dtype)


def dft_apply(xr, xi, f_blk, *, tw=None, pre_tw=False, post_tw=False, filt=None, g_blk=None, out_dtype=_BF16):
    n_a, k, c = xr.shape
    real_input = xi is None
    m1 = f_blk.shape[0] // 2
    m_out = g_blk.shape[0] // 2 if g_blk is not None else m1
    ct = min(HY_CT, c)
    slabs = max(1, min(n_a, 1024 // max(k, m_out)))
    slab_spec = lambda rows: pl.BlockSpec((slabs, rows, ct), lambda j, a: (a, 0, j))
    whole = lambda arr: pl.BlockSpec(arr.shape, lambda j, a: (0,) * arr.ndim)
    in_specs, args = [slab_spec(k)], [xr]
    if not real_input:
        in_specs.append(slab_spec(k))
        args.append(xi)
    in_specs.append(whole(f_blk))
    args.append(f_blk)
    if pre_tw or post_tw:
        in_specs.append(pl.BlockSpec((2, slabs, tw.shape[2], V7X_LANES), lambda j, a: (0, a, 0, 0)))
        args.append(tw)
    if filt is not None:
        h_re, h_im, l1, first_block = filt
        h_spec = pl.BlockSpec((slabs, m1, ct), lambda j, a: (a, 0, j + first_block))
        l1_spec = pl.BlockSpec((V7X_SUBLANES, ct), lambda j, a: (0, j + first_block))
        in_specs += [h_spec, h_spec, l1_spec]
        args += [h_re, h_im, l1]
    if g_blk is not None:
        in_specs.append(whole(g_blk))
        args.append(g_blk)
    kern = functools.partial(_dft_kernel, real_input=real_input, pre_tw=pre_tw, has_filter=filt is not None,
                             has_second=g_blk is not None, post_tw=post_tw, slabs=slabs)
    return pl.pallas_call(
        kern,
        grid=(c // ct, n_a // slabs),
        in_specs=in_specs,
        out_specs=[slab_spec(m_out)] * 2,
        out_shape=[jax.ShapeDtypeStruct((n_a, m_out, c), out_dtype)] * 2,
        compiler_params=_cparams("parallel", "parallel"),
        name="hyena_dft",
    )(*args)


def _fft_factors(n_fft):
    n1 = 1 << ((n_fft.bit_length() - 1) // 2)
    return n_fft // n1, n1


def filter_spectrum(taps):
    n_fft, c = taps.shape
    if n_fft <= 512:
        return dft_apply(taps[None], None, _dft_block(n_fft, n_fft, -1.0, real_input=True))
    n1, n2 = _fft_factors(n_fft)
    x = jnp.swapaxes(taps.reshape(n1, n2, c), 0, 1)
    yr, yi = dft_apply(x, None, _dft_block(n1, n1, -1.0, real_input=True))
    yr, yi = jnp.swapaxes(yr, 0, 1), jnp.swapaxes(yi, 0, 1)
    return dft_apply(yr, yi, _dft_block(n2, n2, -1.0), tw=_twiddle_table(n1, n2, -1.0), pre_tw=True)


def long_conv_pair(vr, vi, filt):
    n, c = vr.shape
    n_fft = 2 * n
    if n_fft <= 512:
        f = _dft_block(n_fft, n, -1.0)
        g = _dft_block(n, n_fft, 1.0, scale=1.0 / n_fft)
        yr, yi = dft_apply(vr[None], vi[None], f, filt=filt, g_blk=g)
        return yr[0], yi[0]
    n1, n2 = _fft_factors(n_fft)
    n1h = n1 // 2
    to_slabs = lambda a: jnp.swapaxes(a.reshape(n1h, n2, c), 0, 1)
    yr, yi = dft_apply(to_slabs(vr), to_slabs(vi), _dft_block(n1, n1h, -1.0))
    yr, yi = jnp.swapaxes(yr, 0, 1), jnp.swapaxes(yi, 0, 1)
    tw = _twiddle_table(n1, n2, -1.0)
    zr, zi = dft_apply(yr, yi, _dft_block(n2, n2, -1.0), tw=tw, pre_tw=True, post_tw=True, filt=filt,
                       g_blk=_dft_block(n2, n2, 1.0))
    zr, zi = jnp.swapaxes(zr, 0, 1), jnp.swapaxes(zi, 0, 1)
    outr, outi = dft_apply(zr, zi, _dft_block(n1h, n1, 1.0, scale=1.0 / n_fft))
    back = lambda a: jnp.swapaxes(a, 0, 1).reshape(n, c)
    return back(outr), back(outi)


def _gate_kernel(x_ref, conv_ref, v_ref, skip_ref, o_ref):
    v = v_ref[...].astype(_F32)
    o_ref[...] = (x_ref[...].astype(_F32) * (conv_ref[...].astype(_F32) + skip_ref[...] * v)).astype(o_ref.dtype)


def hyena_gate(xg, conv, v, skip, layer, order):
    bsz, n, _ = xg.shape
    rows = min(ROW_TILE, n)
    row = pl.BlockSpec((None, rows, HY_W), lambda b, i: (b, i, 0))
    return pl.pallas_call(
        _gate_kernel,
        grid=(bsz, n // rows),
        in_specs=[row, row, row, pl.BlockSpec((None, None, 1, HY_W), lambda b, i: (layer, order, 0, 0))],
        out_specs=row,
        out_shape=jax.ShapeDtypeStruct((bsz, n, HY_W), _BF16),
        compiler_params=_cparams("parallel", "parallel"),
        name="hyena_gate",
    )(xg, conv, v, skip)


def hyena_segment(v, x1, x2, w1, b1, w2, b2, w3, freq, decay, skip, layer):
    bsz, n, _ = v.shape
    assert bsz == 2, "the two batch elements are packed into one complex transform"
    taps, l1 = hyena_filter_taps(n, w1, b1, w2, b2, w3, freq, decay, layer)
    hr, hi = filter_spectrum(taps)
    u = v
    for o, xg in enumerate((x1, x2)):
        filt = (hr, hi, l1, o * (HY_W // min(HY_CT, HY_W)))
        c0, c1 = long_conv_pair(u[0], u[1], filt)
        u = hyena_gate(xg, jnp.stack([c0, c1], 0), u, skip, layer, o)
    return u


def pack_hyena_weights(p):
    depth = p["hy_w1"].shape[0]
    w1 = jnp.pad(p["hy_w1"], ((0, 0), (0, HY_FEAT_PAD - HY_EMB), (0, 0)))
    row = lambda a: a.reshape(depth, 1, -1)
    w3 = p["hy_w3"].reshape(depth, HY_HIDDEN, HY_ORDER, 2, HY_W)
    w3_dir = jnp.transpose(w3, (0, 3, 1, 2, 4)).reshape(depth, 2, HY_HIDDEN, HY_TAP_COLS).astype(_BF16)
    decay_dir = jnp.swapaxes(p["hy_decay"], 1, 2).reshape(depth, 2, 1, HY_TAP_COLS)
    return (w1, row(p["hy_b1"]), p["hy_w2"], row(p["hy_b2"]), w3_dir, row(p["hy_freq"]), decay_dir,
            p["hy_skip"].reshape(depth, HY_ORDER, 1, HY_W))


def kernel(x, c, ctx, c_ctx, w_mod, b_mod, norm1, norm2, w_in, ssm_lam_re, ssm_lam_im, ssm_log_dt, ssm_b_re, ssm_b_im, ssm_c_re, ssm_c_im, ssm_d, ssm_w_glu, ssm_b_glu, mla_q_norm, mla_w_uq, mla_kv_norm, mla_w_ukv, qk_norm_q, qk_norm_k, hy_conv_w, hy_conv_b, hy_w1, hy_b1, hy_w2, hy_b2, hy_w3, hy_freq, hy_decay, hy_skip, w_branch, w_out, router_w, router_bias, moe_w_gate, moe_w_up, moe_w_down):
    depth = w_mod.shape[0]
    bsz, n_lat, _ = x.shape
    n_ctx = ctx.shape[1]
    assert n_ctx % ROW_TILE == 0 and n_lat % ROW_TILE == 0 and ROW_TILE == ATT_TILE
    n_ctx_tiles = n_ctx // ROW_TILE
    row = lambda a: a.reshape(depth, 1, -1)

    mod_tab = modulation_table(c, c_ctx, w_mod, b_mod)
    w_in_used = pack_w_in(w_in)
    w_gates = w_in[:, :, _IN_GATES:].astype(_BF16)
    w_branch_b, w_out_b = w_branch.astype(_BF16), w_out.astype(_BF16)
    moe_b = (moe_w_gate.astype(_BF16), moe_w_up.astype(_BF16), moe_w_down.astype(_BF16))
    rw_hi, rw_lo, rb_col = split_router(router_w, router_bias)
    norm1_r, norm2_r = row(norm1), row(norm2)
    s5_consts, s5_bd, s5_cd = pack_s5(*s5_discretize(ssm_lam_re, ssm_lam_im, ssm_log_dt, ssm_b_re, ssm_b_im),
                                      ssm_c_re, ssm_c_im)
    s5_finish = (row(ssm_d), ssm_w_glu.astype(_BF16), row(ssm_b_glu))
    mla_packed = pack_mla_weights(mla_w_uq, mla_w_ukv, qk_norm_q, qk_norm_k)
    mla_tables = rope_tables(n_ctx, n_lat)
    hy_w = pack_hyena_weights(dict(hy_w1=hy_w1, hy_b1=hy_b1, hy_w2=hy_w2, hy_b2=hy_b2, hy_w3=hy_w3, hy_freq=hy_freq,
                                   hy_decay=hy_decay, hy_skip=hy_skip))
    hy_conv_b_r = row(hy_conv_b)

    xcat = jnp.concatenate([ctx, x], 1)
    for l in range(depth):
        u, q_lat, kv_lat, hy, k_rope = input_projection(xcat, mod_tab, norm1_r, w_in_used, l, n_ctx_tiles)

        y_fwd = s5_scan(u, s5_consts, s5_bd, s5_cd, l, n_ctx_tiles, False)
        y_ssm = s5_scan(u, s5_consts, s5_bd, s5_cd, l, n_ctx_tiles, True, (y_fwd,) + s5_finish)

        q_t, k, v_t = mla_prepare(q_lat, kv_lat, k_rope, row(mla_q_norm), row(mla_kv_norm), mla_packed, mla_tables, l)
        y_att = jnp.swapaxes(flash_attention(q_t, k, v_t, n_ctx_tiles), 1, 2)

        v_c, x1_c, x2_c, v_l, x1_l, x2_l = hyena_short_conv(hy, hy_conv_w, hy_conv_b_r, l, n_ctx)
        hy_lat = hyena_segment(v_l, x1_l, x2_l, *hy_w, l)
        if l < depth - 1:
            hy_ctx = hyena_segment(v_c, x1_c, x2_c, *hy_w, l)
        else:
            hy_ctx = jnp.zeros((bsz, n_ctx, HY_W), hy_lat.dtype)

        x_new, hl, route, counts = merge_and_route(xcat, y_ssm, y_att, hy_ctx, hy_lat, mod_tab, norm1_r, norm2_r,
                                                   w_gates, w_branch_b, w_out_b, rw_hi, rw_lo, rb_col, l, n_ctx_tiles)
        xcat = moe_block(x_new, hl, route, counts, mod_tab, *moe_b, l, n_ctx_tiles)
    return xcat[:, n_ctx:]
```

```python
import functools
import math

import numpy as np
import jax
import jax.numpy as jnp
from jax import lax
from jax.experimental import pallas as pl
from jax.experimental.pallas import tpu as pltpu

D_MODEL = 1024
EPS = 1e-6
GRID_W = 64
SSM_W = 512
SSM_GROUP = 16
SSM_GROUPS = SSM_W // SSM_GROUP
SSM_STATE = 64
MLA_HEADS = 8
MLA_Q_RANK = 384
MLA_KV_RANK = 256
MLA_NOPE = 64
MLA_ROPE = 32
MLA_V = 64
MLA_QK = MLA_NOPE + MLA_ROPE
ROPE_THETA = 10000.0
HY_W = 512
HY_ORDER = 2
HY_BANDS = 16
HY_EMB = 1 + 2 * HY_BANDS
HY_HIDDEN = 64
N_BRANCH = 3
IN_SPLITS = (SSM_W, MLA_Q_RANK, MLA_KV_RANK, MLA_ROPE, 3 * HY_W)
N_EXPERTS = 16
N_EXPERT_GROUPS = 4
EXPERTS_PER_GROUP = N_EXPERTS // N_EXPERT_GROUPS
D_EXPERT = 512

V7X_LANES = 128
V7X_SUBLANES = 8
V7X_VMEM_LIMIT = 56 * 1024 * 1024

ROW_TILE = 256
MOE_TILE = 256
HL_ROW_FACTOR = 2
HEAD_PAD = 128

_F32 = jnp.float32
_BF16 = jnp.bfloat16


def _cparams(*sem):
    return pltpu.CompilerParams(dimension_semantics=sem, vmem_limit_bytes=V7X_VMEM_LIMIT)


def _silu(v):
    return v * jax.nn.sigmoid(v)


def _modulated_norm(x, g, shift, scale):
    y = x * lax.rsqrt(jnp.mean(x * x, -1, keepdims=True) + EPS)
    return (y * g) * (1.0 + scale) + shift


def _dot(a, b):
    return jnp.dot(a, b, preferred_element_type=_F32)


def _mod_kernel(c_ref, w_ref, b_ref, o_ref):
    s = _silu(c_ref[...])
    o_ref[...] = jnp.dot(s, w_ref[...], preferred_element_type=_F32, precision=lax.Precision.HIGHEST) + b_ref[...]


def modulation_table(c, c_ctx, w_mod, b_mod):
    depth = w_mod.shape[0]
    bsz = c.shape[0]
    rows = V7X_SUBLANES * pl.cdiv(bsz + 1, V7X_SUBLANES)
    cvec = jnp.concatenate([c, c_ctx[None, :], jnp.zeros((rows - bsz - 1, D_MODEL), _F32)], 0)
    out = pl.pallas_call(
        _mod_kernel,
        grid=(depth, 6),
        in_specs=[
            pl.BlockSpec((rows, D_MODEL), lambda l, j: (0, 0)),
            pl.BlockSpec((None, D_MODEL, D_MODEL), lambda l, j: (l, 0, j)),
            pl.BlockSpec((None, 1, D_MODEL), lambda l, j: (l, 0, j)),
        ],
        out_specs=pl.BlockSpec((None, rows, D_MODEL), lambda l, j: (l, 0, j)),
        out_shape=jax.ShapeDtypeStruct((depth, rows, 6 * D_MODEL), _F32),
        compiler_params=_cparams("parallel", "parallel"),
        name="modulation",
    )(cvec, w_mod, b_mod.reshape(depth, 1, 6 * D_MODEL))
    m = out.reshape(depth, rows, 6, D_MODEL)
    lat = m[:, :bsz]
    ctx = jnp.broadcast_to(m[:, bsz:bsz + 1], lat.shape)
    tab = jnp.stack([ctx, lat], 2)
    return jnp.pad(tab, ((0, 0), (0, 0), (0, 0), (0, V7X_SUBLANES - 6), (0, 0)))


def _mod_spec(layer, n_ctx_tiles):
    return pl.BlockSpec((None, None, None, V7X_SUBLANES, D_MODEL),
                        lambda b, i: (layer, b, jnp.where(i >= n_ctx_tiles, 1, 0), 0, 0))


_IN_OFF = (0, SSM_W, SSM_W + MLA_Q_RANK, SSM_W + MLA_Q_RANK + MLA_KV_RANK)
_IN_HY = _IN_OFF[3]
_IN_ROPE = _IN_HY + 3 * HY_W
_IN_USED = _IN_ROPE + MLA_ROPE
_IN_GATES = sum(IN_SPLITS)


def pack_w_in(w_in):
    a = SSM_W + MLA_Q_RANK + MLA_KV_RANK
    return jnp.concatenate([w_in[:, :, :a], w_in[:, :, a + MLA_ROPE:_IN_GATES], w_in[:, :, a:a + MLA_ROPE]],
                           -1).astype(_BF16)


def split_router(router_w, router_bias):
    rw = router_w.T
    hi = rw.astype(_BF16)
    lo = (rw - hi.astype(_F32)).astype(_BF16)
    return hi, lo, router_bias.reshape(N_EXPERTS, 1).astype(_F32)


def _inproj_kernel(x_ref, mod_ref, g_ref, w_ref, u_ref, q_ref, kv_ref, hy_ref, kr_ref):
    xn = _modulated_norm(x_ref[...], g_ref[...], mod_ref[0:1, :], mod_ref[1:2, :]).astype(_BF16)
    u_ref[...] = _dot(xn, w_ref[:, _IN_OFF[0]:_IN_OFF[1]]).astype(u_ref.dtype)
    q_ref[...] = _dot(xn, w_ref[:, _IN_OFF[1]:_IN_OFF[2]]).astype(q_ref.dtype)
    kv_ref[...] = _dot(xn, w_ref[:, _IN_OFF[2]:_IN_OFF[3]]).astype(kv_ref.dtype)
    hy_ref[...] = _dot(xn, w_ref[:, _IN_HY:_IN_ROPE]).astype(hy_ref.dtype)
    kr_ref[...] = _dot(xn, w_ref[:, _IN_ROPE:_IN_USED]).astype(kr_ref.dtype)


def input_projection(x, mod_tab, norm1, w_in_used, layer, n_ctx_tiles):
    bsz, t, _ = x.shape
    widths = (SSM_W, MLA_Q_RANK, MLA_KV_RANK, 3 * HY_W, MLA_ROPE)
    row = lambda w: pl.BlockSpec((None, ROW_TILE, w), lambda b, i: (b, i, 0))
    return pl.pallas_call(
        _inproj_kernel,
        grid=(bsz, t // ROW_TILE),
        in_specs=[
            row(D_MODEL),
            _mod_spec(layer, n_ctx_tiles),
            pl.BlockSpec((None, 1, D_MODEL), lambda b, i: (layer, 0, 0)),
            pl.BlockSpec((None, D_MODEL, _IN_USED), lambda b, i: (layer, 0, 0)),
        ],
        out_specs=[row(w) for w in widths],
        out_shape=[jax.ShapeDtypeStruct((bsz, t, w), _BF16) for w in widths],
        compiler_params=_cparams("parallel", "parallel"),
        name="input_projection",
    )(x, mod_tab, norm1, w_in_used)


def _max4(a, b, c, d):
    hi1, lo1 = jnp.maximum(a, b), jnp.minimum(a, b)
    hi2, lo2 = jnp.maximum(c, d), jnp.minimum(c, d)
    return jnp.maximum(hi1, hi2), jnp.maximum(jnp.minimum(hi1, hi2), jnp.maximum(lo1, lo2))


def _first_argmax(vals):
    best, idx = vals[0], jnp.zeros(vals[0].shape, jnp.int32)
    for j in range(1, len(vals)):
        upd = vals[j] > best
        idx = jnp.where(upd, j, idx)
        best = jnp.where(upd, vals[j], best)
    return idx


def _pick(vals, idx):
    out = vals[-1]
    for j in range(len(vals) - 2, -1, -1):
        out = jnp.where(idx == j, vals[j], out)
    return out


def _route_rows(logits_t, bias_col):
    s = jax.nn.sigmoid(logits_t)
    sel = s + bias_col
    s_rows = [s[e:e + 1, :] for e in range(N_EXPERTS)]
    sel_rows = [sel[e:e + 1, :] for e in range(N_EXPERTS)]
    gsum = []
    for g in range(N_EXPERT_GROUPS):
        top, second = _max4(*sel_rows[EXPERTS_PER_GROUP * g:EXPERTS_PER_GROUP * (g + 1)])
        gsum.append(top + second)
    gidx = _first_argmax(gsum)
    v = [_pick([sel_rows[EXPERTS_PER_GROUP * g + j] for g in range(N_EXPERT_GROUPS)], gidx)
         for j in range(EXPERTS_PER_GROUP)]
    u = [_pick([s_rows[EXPERTS_PER_GROUP * g + j] for g in range(N_EXPERT_GROUPS)], gidx)
         for j in range(EXPERTS_PER_GROUP)]
    i1 = _first_argmax(v)
    neg = jnp.full(v[0].shape, -jnp.inf, _F32)
    i2 = _first_argmax([jnp.where(i1 == j, neg, v[j]) for j in range(EXPERTS_PER_GROUP)])
    wa, wb = _pick(u, i1), _pick(u, i2)
    tot = wa + wb
    return EXPERTS_PER_GROUP * gidx + i1, EXPERTS_PER_GROUP * gidx + i2, wa / tot, wb / tot


def _expert_ranks(e0, e1, tri_ref, run_ref):
    n = e0.shape[1]
    rows = lax.broadcasted_iota(jnp.int32, (N_EXPERTS, n), 0)
    oh0, oh1 = rows == e0, rows == e1
    c = jnp.where(oh0, 1.0, 0.0) + jnp.where(oh1, 1.0, 0.0)
    before = _dot(c.astype(_BF16), tri_ref[...])
    run = run_ref[...]
    base = before + jnp.concatenate([run] * (n // V7X_LANES), 1)
    rank0 = jnp.sum(jnp.where(oh0, base, 0.0), 0, keepdims=True)
    rank1 = jnp.sum(jnp.where(oh1, base, 0.0), 0, keepdims=True)
    run_ref[...] = run + jnp.broadcast_to(jnp.sum(c, 1, keepdims=True), run.shape)
    return rank0, rank1


def _merge_kernel(x_ref, ys_ref, yac_ref, yal_ref, yhc_ref, yhl_ref, mod_ref, g1_ref, g2_ref, wg_ref, wb_ref, wo_ref,
                  rwh_ref, rwl_ref, rb_ref, tri_ref, xo_ref, hl_ref, rt_ref, cnt_ref, run_ref, *, n_ctx_tiles):
    x = x_ref[...]
    xn = _modulated_norm(x, g1_ref[...], mod_ref[0:1, :], mod_ref[1:2, :]).astype(_BF16)
    in_ctx = pl.program_id(1) < n_ctx_tiles
    y_att = jnp.where(in_ctx, yac_ref[...], yal_ref[...])
    y_hy = jnp.where(in_ctx, yhc_ref[...], yhl_ref[...])
    m = None
    for k, y in enumerate((ys_ref[...], y_att, y_hy)):
        gate = jax.nn.sigmoid(_dot(xn, wg_ref[:, k * D_MODEL:(k + 1) * D_MODEL]))
        term = gate * _dot(y, wb_ref[k])
        m = term if m is None else m + term
    x_new = x + mod_ref[2:3, :] * _dot(m.astype(_BF16), wo_ref[...])
    xo_ref[...] = x_new
    hl = _modulated_norm(x_new, g2_ref[...], mod_ref[3:4, :], mod_ref[4:5, :])
    hl_hi = hl.astype(_BF16)
    hl_ref[...] = hl_hi
    hl_lo = (hl - hl_hi.astype(_F32)).astype(_BF16)
    nt = (((1,), (1,)), ((), ()))
    logits_t = (lax.dot_general(rwh_ref[...], hl_hi, nt, preferred_element_type=_F32)
                + lax.dot_general(rwl_ref[...], hl_hi, nt, preferred_element_type=_F32)
                + lax.dot_general(rwh_ref[...], hl_lo, nt, preferred_element_type=_F32))
    e0, e1, w0, w1 = _route_rows(logits_t, rb_ref[...])

    @pl.when((pl.program_id(0) == 0) & (pl.program_id(1) == 0))
    def _():
        run_ref[...] = jnp.zeros_like(run_ref)

    rank0, rank1 = _expert_ranks(e0, e1, tri_ref, run_ref)
    cnt_ref[...] = run_ref[...]
    zero = jnp.zeros_like(w0)
    rt_ref[...] = jnp.concatenate([e0.astype(_F32), e1.astype(_F32), w0, w1, rank0, rank1, zero, zero], 0)


def merge_and_route(x, y_ssm, att_ctx, att_lat, hy_ctx, hy_lat, mod_tab, norm1, norm2, w_gates, w_branch, w_out,
                    rw_hi, rw_lo, rb_col, layer, n_ctx_tiles):
    bsz, t, _ = x.shape
    row = lambda w: pl.BlockSpec((None, ROW_TILE, w), lambda b, i: (b, i, 0))
    whole = lambda *shape: pl.BlockSpec(shape, lambda b, i: (0,) * len(shape))
    ctx_row = lambda w: pl.BlockSpec((None, ROW_TILE, w), lambda b, i: (b, jnp.minimum(i, n_ctx_tiles - 1), 0))
    lat_row = lambda w: pl.BlockSpec((None, ROW_TILE, w), lambda b, i: (b, jnp.maximum(i - n_ctx_tiles, 0), 0))
    return pl.pallas_call(
        functools.partial(_merge_kernel, n_ctx_tiles=n_ctx_tiles),
        grid=(bsz, t // ROW_TILE),
        in_specs=[
            row(D_MODEL), row(SSM_W),
            ctx_row(MLA_HEADS * MLA_V), lat_row(MLA_HEADS * MLA_V), ctx_row(HY_W), lat_row(HY_W),
            _mod_spec(layer, n_ctx_tiles),
            pl.BlockSpec((None, 1, D_MODEL), lambda b, i: (layer, 0, 0)),
            pl.BlockSpec((None, 1, D_MODEL), lambda b, i: (layer, 0, 0)),
            pl.BlockSpec((None, D_MODEL, N_BRANCH * D_MODEL), lambda b, i: (layer, 0, 0)),
            pl.BlockSpec((None, N_BRANCH, SSM_W, D_MODEL), lambda b, i: (layer, 0, 0, 0)),
            pl.BlockSpec((None, D_MODEL, D_MODEL), lambda b, i: (layer, 0, 0)),
            whole(N_EXPERTS, D_MODEL), whole(N_EXPERTS, D_MODEL), whole(N_EXPERTS, 1),
            whole(ROW_TILE, ROW_TILE),
        ],
        out_specs=[row(D_MODEL), row(D_MODEL),
                   pl.BlockSpec((None, V7X_SUBLANES, ROW_TILE), lambda b, i: (b, 0, i)),
                   whole(N_EXPERTS, V7X_LANES)],
        out_shape=[jax.ShapeDtypeStruct((bsz, t, D_MODEL), _F32),
                   jax.ShapeDtypeStruct((bsz, t, D_MODEL), _BF16),
                   jax.ShapeDtypeStruct((bsz, V7X_SUBLANES, t), _F32),
                   jax.ShapeDtypeStruct((N_EXPERTS, V7X_LANES), _F32)],
        scratch_shapes=[pltpu.VMEM((N_EXPERTS, V7X_LANES), _F32)],
        compiler_params=_cparams("arbitrary", "arbitrary"),
        name="merge_and_route",
    )(x, y_ssm, att_ctx, att_lat, hy_ctx, hy_lat, mod_tab, norm1, norm2, w_gates, w_branch, w_out, rw_hi, rw_lo,
      rb_col, jnp.triu(jnp.ones((ROW_TILE, ROW_TILE), _BF16), 1))


def _moe_kernel(te_ref, tv_ref, xs_ref, wg_ref, wu_ref, wd_ref, o_ref):
    del te_ref

    @pl.when(tv_ref[pl.program_id(0)] > 0)
    def _():
        h = xs_ref[...]
        act = (_silu(_dot(h, wg_ref[...])) * _dot(h, wu_ref[...])).astype(_BF16)
        o_ref[...] = _dot(act, wd_ref[...]).astype(o_ref.dtype)

    @pl.when(tv_ref[pl.program_id(0)] == 0)
    def _():
        o_ref[...] = jnp.zeros_like(o_ref)


def grouped_experts(xs, tile_expert, tile_valid, w_gate, w_up, w_down, layer):
    rows = xs.shape[0]
    n_tiles = rows // MOE_TILE
    wspec = lambda a, b: pl.BlockSpec((None, None, a, b), lambda i, te, tv: (layer, te[i], 0, 0))
    return pl.pallas_call(
        _moe_kernel,
        grid_spec=pltpu.PrefetchScalarGridSpec(
            num_scalar_prefetch=2,
            grid=(n_tiles,),
            in_specs=[
                pl.BlockSpec((MOE_TILE, D_MODEL), lambda i, te, tv: (i, 0)),
                wspec(D_MODEL, D_EXPERT), wspec(D_MODEL, D_EXPERT), wspec(D_EXPERT, D_MODEL),
            ],
            out_specs=pl.BlockSpec((MOE_TILE, D_MODEL), lambda i, te, tv: (i, 0)),
        ),
        out_shape=jax.ShapeDtypeStruct((rows, D_MODEL), _BF16),
        compiler_params=_cparams("arbitrary"),
        name="grouped_experts",
    )(tile_expert, tile_valid, xs, w_gate, w_up, w_down)


def expert_dispatch_plan(route, counts):
    bsz, _, t = route.shape
    n = bsz * t
    slots = lambda r0: jnp.stack([route[:, r0, :], route[:, r0 + 1, :]], -1).reshape(n * 2).astype(jnp.int32)
    e, rank = slots(0), slots(4)
    count = counts[:, 0].astype(jnp.int32)
    tiles = (count + MOE_TILE - 1) // MOE_TILE
    tile_start = jnp.cumsum(tiles) - tiles
    dest = tile_start[e] * MOE_TILE + rank
    rows = 2 * n + N_EXPERTS * MOE_TILE
    n_tiles = rows // MOE_TILE
    tile_ids = jnp.arange(n_tiles, dtype=jnp.int32)
    ends = jnp.cumsum(tiles)
    tile_expert = jnp.minimum(jnp.sum(tile_ids[:, None] >= ends[None, :], 1), N_EXPERTS - 1)
    tile_valid = (tile_ids < jnp.sum(tiles)).astype(jnp.int32)
    src = jnp.zeros((rows,), jnp.int32).at[dest].set(jnp.arange(2 * n, dtype=jnp.int32) // 2)
    dest2 = dest.reshape(n, 2)
    return src, tile_expert.astype(jnp.int32), tile_valid, dest2[:, 0], dest2[:, 1]


def _combine_kernel(x_ref, a_ref, b_ref, wa_ref, wb_ref, mod_ref, o_ref):
    wa, wb = wa_ref[...], wb_ref[...]
    for j in range(D_MODEL // V7X_LANES):
        sl = slice(j * V7X_LANES, (j + 1) * V7X_LANES)
        y = wa * a_ref[:, sl].astype(_F32) + wb * b_ref[:, sl].astype(_F32)
        o_ref[:, sl] = x_ref[:, sl] + mod_ref[5:6, sl] * y


def combine_experts(x, ya, yb, wa, wb, mod_tab, layer, n_ctx_tiles):
    bsz, t, _ = x.shape
    row = lambda w: pl.BlockSpec((None, ROW_TILE, w), lambda b, i: (b, i, 0))
    return pl.pallas_call(
        _combine_kernel,
        grid=(bsz, t // ROW_TILE),
        in_specs=[row(D_MODEL), row(D_MODEL), row(D_MODEL), row(V7X_LANES), row(V7X_LANES),
                  _mod_spec(layer, n_ctx_tiles)],
        out_specs=row(D_MODEL),
        out_shape=jax.ShapeDtypeStruct(x.shape, _F32),
        compiler_params=_cparams("parallel", "parallel"),
        name="combine_experts",
    )(x, ya, yb, wa, wb, mod_tab)


def moe_block(x_new, hl, route, counts, mod_tab, w_gate, w_up, w_down, layer, n_ctx_tiles):
    bsz, t, _ = x_new.shape
    src, tile_expert, tile_valid, d0, d1 = expert_dispatch_plan(route, counts)
    hl_rows = jnp.pad(hl.reshape(bsz * t, D_MODEL), ((0, (HL_ROW_FACTOR - 1) * bsz * t), (0, 0)))
    xs = jnp.take(hl_rows, src, axis=0, mode="clip")
    ys = grouped_experts(xs, tile_expert, tile_valid, w_gate, w_up, w_down, layer)
    ya = jnp.take(ys, d0, axis=0, mode="clip").reshape(bsz, t, D_MODEL)
    yb = jnp.take(ys, d1, axis=0, mode="clip").reshape(bsz, t, D_MODEL)
    lanes = lambda r: jnp.broadcast_to(route[:, r, :, None], (bsz, t, V7X_LANES))
    return combine_experts(x_new, ya, yb, lanes(2), lanes(3), mod_tab, layer, n_ctx_tiles)


ATT_TILE = 256
ATT_CHUNK_TILES = 4
ATT_UNROLL = 8
ATT_LATENT_TQ = 512
ATT_MAX_BOUND = 60.0
V_ROWS = MLA_V + 16
Q_SCALE = MLA_QK ** -0.5 * math.log2(math.e)
_ROPE_LO = MLA_NOPE
_ROPE_HALF = MLA_ROPE // 2


def rope_tables(n_ctx, n_lat):
    rows = n_lat // GRID_W
    row = jnp.broadcast_to(jnp.arange(rows, dtype=_F32)[:, None], (rows, GRID_W)).reshape(n_lat)
    col = jnp.broadcast_to(jnp.arange(GRID_W, dtype=_F32)[None, :], (rows, GRID_W)).reshape(n_lat)
    n_f = MLA_ROPE // 4
    inv = ROPE_THETA ** (-jnp.arange(n_f, dtype=_F32) / n_f)
    ang = jnp.concatenate([row[:, None] * inv, col[:, None] * inv], -1)
    cos = jnp.concatenate([jnp.ones((n_ctx, _ROPE_HALF), _F32), jnp.cos(ang)], 0)
    sin = jnp.concatenate([jnp.zeros((n_ctx, _ROPE_HALF), _F32), jnp.sin(ang)], 0)
    t = n_ctx + n_lat
    ones, zeros = jnp.ones((t, MLA_NOPE), _F32), jnp.zeros((t, MLA_NOPE), _F32)
    tail1, tail0 = jnp.ones((t, HEAD_PAD - MLA_QK), _F32), jnp.zeros((t, HEAD_PAD - MLA_QK), _F32)
    cos_p = jnp.concatenate([ones, cos, cos, tail1], -1)
    sin_p = jnp.concatenate([zeros, -sin, sin, tail0], -1)
    return cos.T, sin.T, cos_p, sin_p


def pack_mla_weights(w_uq, w_ukv, qk_norm_q, qk_norm_k):
    depth = w_uq.shape[0]
    pad = HEAD_PAD - MLA_QK
    wq = w_uq.reshape(depth, MLA_Q_RANK, MLA_HEADS, MLA_QK)
    wq = jnp.pad(wq, ((0, 0), (0, 0), (0, 0), (0, pad))).reshape(depth, MLA_Q_RANK, MLA_HEADS * HEAD_PAD)
    wq_t = jnp.swapaxes(wq, 1, 2).astype(_BF16)
    wkv = w_ukv.reshape(depth, MLA_KV_RANK, MLA_HEADS, MLA_NOPE + MLA_V)
    wk = jnp.pad(wkv[..., :MLA_NOPE], ((0, 0), (0, 0), (0, 0), (0, HEAD_PAD - MLA_NOPE)))
    wk = wk.reshape(depth, MLA_KV_RANK, MLA_HEADS * HEAD_PAD).astype(_BF16)
    wv_t = jnp.swapaxes(wkv[..., MLA_NOPE:].reshape(depth, MLA_KV_RANK, MLA_HEADS * MLA_V), 1, 2).astype(_BF16)
    place = jnp.zeros((MLA_ROPE, HEAD_PAD), _F32).at[jnp.arange(MLA_ROPE), _ROPE_LO + jnp.arange(MLA_ROPE)].set(1.0)
    place = jnp.tile(place, (1, MLA_HEADS)).astype(_BF16)
    gq = jnp.pad(qk_norm_q, ((0, 0), (0, pad)))
    gq_col = jnp.broadcast_to(gq[:, :, None], (depth, HEAD_PAD, ATT_TILE)).astype(_F32)
    gk_row = jnp.pad(qk_norm_k, ((0, 0), (0, pad))).reshape(depth, 1, HEAD_PAD).astype(_F32)
    return wq_t, wk, wv_t, place, gq_col, gk_row


def _mla_prep_kernel(ql_ref, kvl_ref, kr_ref, qn_ref, kvn_ref, wq_ref, wk_ref, wv_ref, place_ref, gq_ref, gk_ref,
                     cos_t_ref, sin_t_ref, cos_p_ref, sin_p_ref, qt_ref, k_ref, vt_ref):
    nt = (((1,), (1,)), ((), ()))
    ql = ql_ref[...].astype(_F32)
    qn = (ql * lax.rsqrt(jnp.mean(ql * ql, -1, keepdims=True) + EPS) * qn_ref[...]).astype(_BF16)
    kvl = kvl_ref[...].astype(_F32)
    kvn = (kvl * lax.rsqrt(jnp.mean(kvl * kvl, -1, keepdims=True) + EPS) * kvn_ref[...]).astype(_BF16)

    q_t = lax.dot_general(wq_ref[...], qn, nt, preferred_element_type=_F32)
    cos_t, sin_t = cos_t_ref[...], sin_t_ref[...]
    scale = Q_SCALE
    lo, mid, hi = _ROPE_LO, _ROPE_LO + _ROPE_HALF, _ROPE_LO + MLA_ROPE
    for h in range(MLA_HEADS):
        blk = q_t[h * HEAD_PAD:(h + 1) * HEAD_PAD, :]
        ms = jnp.sum(blk * blk, 0, keepdims=True) * (1.0 / MLA_QK)
        y = blk * (lax.rsqrt(ms + EPS) * scale) * gq_ref[...]
        r1, r2 = y[lo:mid, :], y[mid:hi, :]
        rot = jnp.concatenate([y[:lo, :], r1 * cos_t - r2 * sin_t, r2 * cos_t + r1 * sin_t, y[hi:, :]], 0)
        qt_ref[h] = rot.astype(qt_ref.dtype)

    k_pre = _dot(kvn, wk_ref[...]) + _dot(kr_ref[...], place_ref[...])
    cos_p, sin_p = cos_p_ref[...], sin_p_ref[...]
    lane = lax.broadcasted_iota(jnp.int32, cos_p.shape, 1)
    for h in range(MLA_HEADS):
        blk = k_pre[:, h * HEAD_PAD:(h + 1) * HEAD_PAD]
        ms = jnp.sum(blk * blk, -1, keepdims=True) * (1.0 / MLA_QK)
        y = blk * lax.rsqrt(ms + EPS) * gk_ref[...]
        swap = jnp.where(lane < mid, pltpu.roll(y, HEAD_PAD - _ROPE_HALF, 1), pltpu.roll(y, _ROPE_HALF, 1))
        k_ref[h] = (y * cos_p + swap * sin_p).astype(k_ref.dtype)

    v_t = lax.dot_general(wv_ref[...], kvn, nt, preferred_element_type=_F32)
    ones = jnp.ones((V_ROWS - MLA_V, v_t.shape[1]), _F32)
    for h in range(MLA_HEADS):
        vt_ref[h] = jnp.concatenate([v_t[h * MLA_V:(h + 1) * MLA_V, :], ones], 0).astype(vt_ref.dtype)


def mla_prepare(q_lat, kv_lat, k_rope, q_norm, kv_norm, packed, tables, layer):
    bsz, t, _ = q_lat.shape
    wq_t, wk, wv_t, place, gq_col, gk_row = packed
    cos_t, sin_t, cos_p, sin_p = tables
    n_tiles = t // ATT_TILE
    row = lambda w: pl.BlockSpec((None, ATT_TILE, w), lambda b, i: (b, i, 0))
    lay = lambda *shape: pl.BlockSpec((None,) + shape, lambda b, i: (layer,) + (0,) * len(shape))
    return pl.pallas_call(
        _mla_prep_kernel,
        grid=(bsz, n_tiles),
        in_specs=[
            row(MLA_Q_RANK), row(MLA_KV_RANK), row(MLA_ROPE),
            lay(1, MLA_Q_RANK), lay(1, MLA_KV_RANK),
            lay(MLA_HEADS * HEAD_PAD, MLA_Q_RANK), lay(MLA_KV_RANK, MLA_HEADS * HEAD_PAD),
            lay(MLA_HEADS * MLA_V, MLA_KV_RANK),
            pl.BlockSpec((MLA_ROPE, MLA_HEADS * HEAD_PAD), lambda b, i: (0, 0)),
            lay(HEAD_PAD, ATT_TILE), lay(1, HEAD_PAD),
            pl.BlockSpec((_ROPE_HALF, ATT_TILE), lambda b, i: (0, i)),
            pl.BlockSpec((_ROPE_HALF, ATT_TILE), lambda b, i: (0, i)),
            pl.BlockSpec((ATT_TILE, HEAD_PAD), lambda b, i: (i, 0)),
            pl.BlockSpec((ATT_TILE, HEAD_PAD), lambda b, i: (i, 0)),
        ],
        out_specs=[
            pl.BlockSpec((None, MLA_HEADS, HEAD_PAD, ATT_TILE), lambda b, i: (b, 0, 0, i)),
            pl.BlockSpec((None, MLA_HEADS, ATT_TILE, HEAD_PAD), lambda b, i: (b, 0, i, 0)),
            pl.BlockSpec((None, MLA_HEADS, None, V_ROWS, ATT_TILE), lambda b, i: (b, 0, i, 0, 0)),
        ],
        out_shape=[
            jax.ShapeDtypeStruct((bsz, MLA_HEADS, HEAD_PAD, t), _BF16),
            jax.ShapeDtypeStruct((bsz, MLA_HEADS, t, HEAD_PAD), _BF16),
            jax.ShapeDtypeStruct((bsz, MLA_HEADS, n_tiles, V_ROWS, ATT_TILE), _BF16),
        ],
        compiler_params=_cparams("parallel", "parallel"),
        name="mla_prepare",
    )(q_lat, kv_lat, k_rope, q_norm, kv_norm, wq_t, wk, wv_t, place, gq_col, gk_row, cos_t, sin_t, cos_p, sin_p)


def _attention_scores(s_ref, slot, q_t, k_ref, tile0, n_sub):
    for j in range(n_sub):
        row0 = (tile0 + j) * ATT_TILE
        if not isinstance(row0, int):
            row0 = pl.multiple_of(row0, ATT_TILE)
        s_ref[slot, j * ATT_TILE:(j + 1) * ATT_TILE, :] = _dot(k_ref[pl.ds(row0, ATT_TILE), :], q_t)


def _attention_update(s_ref, slot, vt_ref, tile0, n_sub, carry, shift):
    m, acc = carry
    tiles = [s_ref[slot, j * ATT_TILE:(j + 1) * ATT_TILE, :] for j in range(n_sub)]
    if shift is None:
        m_new = m
        for s_j in tiles:
            m_new = jnp.maximum(m_new, jnp.max(s_j, 0, keepdims=True))
        acc = jnp.exp2(m - m_new) * acc
    else:
        m_new = m
    for j, s_j in enumerate(tiles):
        if shift is None:
            p = jnp.exp2((s_j - m_new).astype(_BF16))
        else:
            p = jnp.exp2(s_j - shift).astype(_BF16)
        acc = acc + _dot(vt_ref[tile0 + j], p)
    return m_new, acc


def _attention_kernel(bound_ref, qt_ref, k_ref, vt_ref, o_ref, s_ref, *, n_ctx_tiles, n_chunks, n_sub, unroll):
    q_t = qt_ref[...]
    tq = q_t.shape[1]
    first = lambda c: n_ctx_tiles + c * n_sub
    bound = bound_ref[0]

    def streamed(shift):
        carry = (jnp.full((1, tq), -jnp.inf, _F32), jnp.zeros((V_ROWS, tq), _F32))
        _attention_scores(s_ref, 1, q_t, k_ref, 0, n_ctx_tiles)
        carry = _attention_update(s_ref, 1, vt_ref, 0, n_ctx_tiles, carry, shift)

        def group(base, cr):
            _attention_scores(s_ref, 0, q_t, k_ref, first(base), n_sub)
            for c in range(unroll):
                if c + 1 < unroll:
                    _attention_scores(s_ref, (c + 1) % 2, q_t, k_ref, first(base + c + 1), n_sub)
                cr = _attention_update(s_ref, c % 2, vt_ref, first(base + c), n_sub, cr, shift)
            return cr

        if n_chunks == unroll:
            carry = group(0, carry)
        elif n_chunks:
            carry = lax.fori_loop(0, n_chunks // unroll, lambda g, c: group(g * unroll, c), carry)
        return carry[1]

    acc = lax.cond(bound <= ATT_MAX_BOUND, lambda _: streamed(bound), lambda _: streamed(None), 0)
    o_ref[...] = (acc[:MLA_V] / acc[MLA_V:MLA_V + 1]).astype(o_ref.dtype)


def _attention_call(bound, q_t, k, v_t, n_ctx_tiles, n_chunks, n_sub, unroll, tq, name):
    bsz, heads, _, n_q = q_t.shape
    t = k.shape[2]
    n_tiles = t // ATT_TILE
    return pl.pallas_call(
        functools.partial(_attention_kernel, n_ctx_tiles=n_ctx_tiles, n_chunks=n_chunks, n_sub=n_sub, unroll=unroll),
        grid=(bsz, heads, n_q // tq),
        in_specs=[
            pl.BlockSpec(memory_space=pltpu.SMEM),
            pl.BlockSpec((None, None, HEAD_PAD, tq), lambda b, h, i: (b, h, 0, i)),
            pl.BlockSpec((None, None, t, HEAD_PAD), lambda b, h, i: (b, h, 0, 0)),
            pl.BlockSpec((None, None, n_tiles, V_ROWS, ATT_TILE), lambda b, h, i: (b, h, 0, 0, 0)),
        ],
        out_specs=pl.BlockSpec((None, MLA_V, tq), lambda b, h, i: (b, h, i)),
        out_shape=jax.ShapeDtypeStruct((bsz, heads * MLA_V, n_q), _BF16),
        scratch_shapes=[pltpu.VMEM((2, max(n_sub, n_ctx_tiles) * ATT_TILE, tq), _F32)],
        compiler_params=_cparams("parallel", "parallel", "arbitrary"),
        name=name,
    )(bound, q_t, k, v_t)


def score_bound(qk_norm_q, qk_norm_k):
    gq = jnp.max(jnp.abs(qk_norm_q), -1)
    gk = jnp.max(jnp.abs(qk_norm_k), -1)
    return (MLA_QK * Q_SCALE * 1.02) * gq * gk


def flash_attention(q_t, k, v_t, n_ctx_tiles, bound):
    t = q_t.shape[3]
    n_ctx = n_ctx_tiles * ATT_TILE
    n_lat_tiles = t // ATT_TILE - n_ctx_tiles
    n_sub = math.gcd(n_lat_tiles, ATT_CHUNK_TILES)
    unroll = math.gcd(n_lat_tiles // n_sub, ATT_UNROLL)
    tq_lat = math.gcd(t - n_ctx, ATT_LATENT_TQ)
    out_ctx = _attention_call(bound, q_t[..., :n_ctx], k, v_t, n_ctx_tiles, 0, n_sub, unroll, ATT_TILE,
                              "attention_context")
    out_lat = _attention_call(bound, q_t[..., n_ctx:], k, v_t, n_ctx_tiles, n_lat_tiles // n_sub, n_sub, unroll,
                              tq_lat, "attention_latent")
    return out_ctx, out_lat


S5_KG = SSM_W // V7X_LANES
S5_GPK = SSM_GROUPS // S5_KG
S5_HALF = S5_GPK * SSM_STATE
S5_COLS = 2 * SSM_GROUPS * SSM_STATE
S5_PAIRS = SSM_GROUPS * SSM_STATE // V7X_LANES
S5_LEVELS = (1, 2, 4)


def _s5_disc_kernel(lr_ref, li_ref, ldt_ref, br_ref, bi_ref, pr_ref, pi_ref, bbr_ref, bbi_ref):
    lr = jnp.minimum(lr_ref[...], -1e-4)
    li = li_ref[...]
    dt = jnp.exp(ldt_ref[...])
    mag = jnp.exp(lr * dt)
    ar, ai = mag * jnp.cos(li * dt), mag * jnp.sin(li * dt)
    den = lr * lr + li * li
    gr = ((ar - 1.0) * lr + ai * li) / den
    gi = (ai * lr - (ar - 1.0) * li) / den
    br, bi = br_ref[...], bi_ref[...]
    bbr_ref[...] = gr * br - gi * bi
    bbi_ref[...] = gr * bi + gi * br
    pr, pi = ar, ai
    for k in range(V7X_SUBLANES):
        pr_ref[k] = pr
        pi_ref[k] = pi
        pr, pi = pr * ar - pi * ai, pr * ai + pi * ar


def s5_discretize(lam_re, lam_im, log_dt, b_re, b_im):
    depth = lam_re.shape[0]
    ld = depth * 2
    g, p, j = SSM_GROUPS, SSM_STATE, SSM_GROUP
    flat = lambda a: a.reshape((ld,) + a.shape[2:])
    g1p = lambda a: a.reshape(ld, g, 1, p)
    ldt = jnp.broadcast_to(flat(log_dt)[:, :, None, None], (ld, g, 1, p))
    b_t = lambda a: jnp.swapaxes(flat(a), 2, 3)
    gp = pl.BlockSpec((None, g, 1, p), lambda i: (i, 0, 0, 0))
    gjp = pl.BlockSpec((None, g, j, p), lambda i: (i, 0, 0, 0))
    pw = pl.BlockSpec((None, V7X_SUBLANES, g, 1, p), lambda i: (i, 0, 0, 0, 0))
    pow_r, pow_i, bbr, bbi = pl.pallas_call(
        _s5_disc_kernel,
        grid=(ld,),
        in_specs=[gp, gp, gp, gjp, gjp],
        out_specs=[pw, pw, gjp, gjp],
        out_shape=[jax.ShapeDtypeStruct((ld, V7X_SUBLANES, g, 1, p), _F32)] * 2
        + [jax.ShapeDtypeStruct((ld, g, j, p), _F32)] * 2,
        compiler_params=_cparams("parallel"),
        name="s5_discretize",
    )(g1p(lam_re), g1p(lam_im), ldt, b_t(b_re), b_t(b_im))
    return pow_r.reshape(ld, V7X_SUBLANES, g, p), pow_i.reshape(ld, V7X_SUBLANES, g, p), bbr, bbi


def pack_s5(pow_r, pow_i, bbr, bbi, c_re, c_im):
    ld = pow_r.shape[0]
    n = SSM_GROUPS * SSM_STATE
    rev = (jnp.arange(ld) % 2 == 1)[:, None, None]
    t = jnp.arange(V7X_SUBLANES)[None, :, None]
    pr, pi = pow_r.reshape(ld, V7X_SUBLANES, n), pow_i.reshape(ld, V7X_SUBLANES, n)
    consts = []
    for s in S5_LEVELS:
        keep = jnp.where(rev, t < V7X_SUBLANES - s, t >= s)
        for a in (pr, pi):
            consts.append(jnp.where(keep, a[:, s - 1:s, :], 0.0))
    for a in (pr, pi):
        consts.append(jnp.where(rev, a[:, ::-1, :], a))
    consts = jnp.stack(consts, 1)
    eye = jnp.eye(S5_GPK, dtype=_F32)
    bb = jnp.stack([bbr, bbi], 1).reshape(ld, 2, S5_KG, S5_GPK, SSM_GROUP, SSM_STATE)
    bd = jnp.einsum("dckgjp,gh->dkgjchp", bb, eye).reshape(ld, S5_KG, V7X_LANES, 2 * S5_HALF).astype(_BF16)
    cc = jnp.stack([c_re, -c_im], 2)
    cc = cc.reshape(ld, 2, S5_KG, S5_GPK, SSM_GROUP, SSM_STATE)
    cd = jnp.einsum("dckgjp,gh->dkcgphj", cc, eye).reshape(ld, S5_KG, 2 * S5_HALF, V7X_LANES).astype(_BF16)
    return consts, bd, cd


def _s5_scan_kernel(u_ref, bd_ref, cd_ref, k_ref, *rest, reverse, finish):
    if finish:
        yf_ref, d_ref, wg_ref, bg_ref, o_ref, bu_ref, carry_ref = rest
    else:
        o_ref, bu_ref, carry_ref = rest
    n_rows = u_ref.shape[0]

    @pl.when(pl.program_id(1) == 0)
    def _():
        carry_ref[...] = jnp.zeros_like(carry_ref)

    u = u_ref[...]
    for kg in range(S5_KG):
        bu_ref[:, kg * 2 * S5_HALF:(kg + 1) * 2 * S5_HALF] = _dot(u[:, kg * V7X_LANES:(kg + 1) * V7X_LANES], bd_ref[kg])

    n_vregs = n_rows // V7X_SUBLANES
    last = 0 if reverse else V7X_SUBLANES - 1

    def step(r, _):
        row = pl.multiple_of((n_vregs - 1 - r if reverse else r) * V7X_SUBLANES, V7X_SUBLANES)
        for jp in range(S5_PAIRS):
            kg, q = divmod(jp, S5_PAIRS // S5_KG)
            cr = kg * 2 * S5_HALF + q * V7X_LANES
            ci = cr + S5_HALF
            cl = jp * V7X_LANES
            xr = bu_ref[pl.ds(row, V7X_SUBLANES), cr:cr + V7X_LANES]
            xi = bu_ref[pl.ds(row, V7X_SUBLANES), ci:ci + V7X_LANES]
            for lvl, s in enumerate(S5_LEVELS):
                shift = V7X_SUBLANES - s if reverse else s
                sr, si = pltpu.roll(xr, shift, 0), pltpu.roll(xi, shift, 0)
                ar = k_ref[2 * lvl, :, cl:cl + V7X_LANES]
                ai = k_ref[2 * lvl + 1, :, cl:cl + V7X_LANES]
                xr, xi = xr + (ar * sr - ai * si), xi + (ar * si + ai * sr)
            pr = k_ref[2 * len(S5_LEVELS), :, cl:cl + V7X_LANES]
            pi = k_ref[2 * len(S5_LEVELS) + 1, :, cl:cl + V7X_LANES]
            hr0 = carry_ref[:, cr:cr + V7X_LANES]
            hi0 = carry_ref[:, ci:ci + V7X_LANES]
            xr, xi = xr + (pr * hr0 - pi * hi0), xi + (pr * hi0 + pi * hr0)
            bu_ref[pl.ds(row, V7X_SUBLANES), cr:cr + V7X_LANES] = xr
            bu_ref[pl.ds(row, V7X_SUBLANES), ci:ci + V7X_LANES] = xi
            carry_ref[:, cr:cr + V7X_LANES] = jnp.broadcast_to(xr[last:last + 1, :], xr.shape)
            carry_ref[:, ci:ci + V7X_LANES] = jnp.broadcast_to(xi[last:last + 1, :], xi.shape)
        return 0

    lax.fori_loop(0, n_vregs, step, 0)

    ys = [_dot(bu_ref[:, kg * 2 * S5_HALF:(kg + 1) * 2 * S5_HALF].astype(_BF16), cd_ref[kg]) for kg in range(S5_KG)]
    y = jnp.concatenate(ys, -1)
    if finish:
        y = y + yf_ref[...] + d_ref[...] * u.astype(_F32)
        z = jax.nn.gelu(y)
        o_ref[...] = (z * jax.nn.sigmoid(_dot(z.astype(_BF16), wg_ref[...]) + bg_ref[...])).astype(o_ref.dtype)
    else:
        o_ref[...] = y


def s5_scan(u, consts, bd, cd, layer, n_ctx_tiles, reverse, finish_args=None):
    bsz, t, _ = u.shape
    n_tiles = t // ROW_TILE
    d = 2 * layer + (1 if reverse else 0)
    if reverse:
        chunk = lambda i: jnp.where(i < n_ctx_tiles, n_ctx_tiles - 1 - i, n_tiles - 1 - (i - n_ctx_tiles))
    else:
        chunk = lambda i: i
    row = lambda w: pl.BlockSpec((None, ROW_TILE, w), lambda b, i: (b, chunk(i), 0))
    lay = lambda *shape: pl.BlockSpec((None,) + shape, lambda b, i: (d,) + (0,) * len(shape))
    in_specs = [row(SSM_W), lay(S5_KG, V7X_LANES, 2 * S5_HALF), lay(S5_KG, 2 * S5_HALF, V7X_LANES),
                lay(2 * len(S5_LEVELS) + 2, V7X_SUBLANES, SSM_GROUPS * SSM_STATE)]
    args = [u, bd, cd, consts]
    finish = finish_args is not None
    if finish:
        y_fwd, d_skip, w_glu, b_glu = finish_args
        lyr = lambda *shape: pl.BlockSpec((None,) + shape, lambda b, i: (layer,) + (0,) * len(shape))
        in_specs += [row(SSM_W), lyr(1, SSM_W), lyr(SSM_W, SSM_W), lyr(1, SSM_W)]
        args += [y_fwd, d_skip, w_glu, b_glu]
    return pl.pallas_call(
        functools.partial(_s5_scan_kernel, reverse=reverse, finish=finish),
        grid=(bsz, n_tiles),
        in_specs=in_specs,
        out_specs=row(SSM_W),
        out_shape=jax.ShapeDtypeStruct((bsz, t, SSM_W), _BF16 if finish else _F32),
        scratch_shapes=[
            pltpu.VMEM((ROW_TILE, S5_COLS), _F32),
            pltpu.VMEM((V7X_SUBLANES, S5_COLS), _F32),
        ],
        compiler_params=_cparams("parallel", "arbitrary"),
        name="s5_scan_bwd" if reverse else "s5_scan_fwd",
    )(*args)


HY_CT = 512
HY_FEAT_PAD = V7X_LANES
HY_TAP_COLS = HY_ORDER * HY_W


def _short_conv_kernel(prev_ref, u_ref, next_ref, w_ref, b_ref, *out_refs, n_ctx, n_total):
    i = pl.program_id(1)
    u = u_ref[...].astype(_F32)
    rows = u.shape[0]
    r = lax.broadcasted_iota(jnp.int32, u.shape, 0)
    g = r + i * rows
    before = jnp.broadcast_to(prev_ref[V7X_SUBLANES - 1:V7X_SUBLANES, :].astype(_F32), u.shape)
    after = jnp.broadcast_to(next_ref[0:1, :].astype(_F32), u.shape)
    up = jnp.where(r == 0, before, pltpu.roll(u, 1, 0))
    un = jnp.where(r == rows - 1, after, pltpu.roll(u, rows - 1, 0))
    up = jnp.where((g == 0) | (g == n_ctx), 0.0, up)
    un = jnp.where((g == n_ctx - 1) | (g == n_total - 1), 0.0, un)
    us = w_ref[0:1, :] * up + w_ref[1:2, :] * u + w_ref[2:3, :] * un + b_ref[...]
    parts = [us[:, k * HY_W:(k + 1) * HY_W] for k in range(3)]

    @pl.when(i * rows < n_ctx)
    def _():
        for ref, part in zip(out_refs[:3], parts):
            ref[...] = part.astype(ref.dtype)

    @pl.when(i * rows >= n_ctx)
    def _():
        for ref, part in zip(out_refs[3:], parts):
            ref[...] = part.astype(ref.dtype)


def hyena_short_conv(hy, conv_w, conv_b, layer, n_ctx):
    bsz, t, width = hy.shape
    per = ROW_TILE // V7X_SUBLANES
    n_halo = t // V7X_SUBLANES
    n_ctx_tiles = n_ctx // ROW_TILE
    row = lambda w: pl.BlockSpec((None, ROW_TILE, w), lambda b, i: (b, i, 0))
    return pl.pallas_call(
        functools.partial(_short_conv_kernel, n_ctx=n_ctx, n_total=t),
        grid=(bsz, t // ROW_TILE),
        in_specs=[
            pl.BlockSpec((None, V7X_SUBLANES, width), lambda b, i: (b, jnp.maximum(i * per - 1, 0), 0)),
            row(width),
            pl.BlockSpec((None, V7X_SUBLANES, width), lambda b, i: (b, jnp.minimum((i + 1) * per, n_halo - 1), 0)),
            pl.BlockSpec((None, conv_w.shape[1], width), lambda b, i: (layer, 0, 0)),
            pl.BlockSpec((None, 1, width), lambda b, i: (layer, 0, 0)),
        ],
        out_specs=[pl.BlockSpec((None, ROW_TILE, HY_W), lambda b, i: (b, jnp.minimum(i, n_ctx_tiles - 1), 0))] * 3
        + [pl.BlockSpec((None, ROW_TILE, HY_W), lambda b, i: (b, jnp.maximum(i - n_ctx_tiles, 0), 0))] * 3,
        out_shape=[jax.ShapeDtypeStruct((bsz, n_ctx, HY_W), _BF16)] * 3
        + [jax.ShapeDtypeStruct((bsz, t - n_ctx, HY_W), _BF16)] * 3,
        compiler_params=_cparams("arbitrary", "arbitrary"),
        name="hyena_short_conv",
    )(hy, hy, hy, conv_w, conv_b)


def hyena_features(n):
    m = jnp.arange(2 * n, dtype=jnp.int32)
    t = jnp.where(m < n, m, jnp.where(m == n, 0, 2 * n - m)).astype(_F32)
    t_norm = t / (n - 1)
    bands = jnp.linspace(1e-4, HY_BANDS - 1, HY_BANDS, dtype=_F32)
    ang = (2.0 * math.pi * t / n)[:, None] * bands[None, :]
    feat = jnp.concatenate([t_norm[:, None], jnp.cos(ang), -jnp.sin(ang)], -1)
    feat = jnp.pad(feat, ((0, 0), (0, HY_FEAT_PAD - HY_EMB)))
    return feat, jnp.broadcast_to(t_norm[:, None], (2 * n, V7X_LANES))


def _filter_kernel(feat_ref, tn_ref, w1_ref, b1_ref, w2_ref, b2_ref, w3_ref, fr_ref, dec_ref, h_ref, l1_ref, *, n):
    i = pl.program_id(0)
    hp = lax.Precision.HIGHEST
    fr = fr_ref[...]
    h = jnp.sin(fr * (jnp.dot(feat_ref[...], w1_ref[...], preferred_element_type=_F32, precision=hp) + b1_ref[...]))
    h = jnp.sin(fr * (jnp.dot(h, w2_ref[...], preferred_element_type=_F32, precision=hp) + b2_ref[...]))
    h = _dot(h.astype(_BF16), w3_ref[...])
    tn = tn_ref[...]
    rows = h.shape[0]
    zero_tap = (lax.broadcasted_iota(jnp.int32, (rows, V7X_LANES), 0) + i * rows) == n

    @pl.when(i == 0)
    def _():
        l1_ref[...] = jnp.zeros_like(l1_ref)

    for c in range(HY_TAP_COLS // V7X_LANES):
        sl = slice(c * V7X_LANES, (c + 1) * V7X_LANES)
        blk = jnp.where(zero_tap, 0.0, h[:, sl] * jnp.exp(-tn * jnp.abs(dec_ref[:, sl])))
        h_ref[:, sl] = blk.astype(h_ref.dtype)
        l1_ref[:, sl] += jnp.sum(jnp.abs(blk).reshape(rows // V7X_SUBLANES, V7X_SUBLANES, V7X_LANES), 0)


def hyena_filter_taps(n, w1, b1, w2, b2, w3_dir, freq, decay_dir, layer):
    feat, tn = hyena_features(n)
    rows = min(ROW_TILE, n)
    half = n // rows
    lay = lambda *shape: pl.BlockSpec((None,) + shape, lambda i: (layer,) + (0,) * len(shape))
    by_dir = lambda *shape: pl.BlockSpec((None, None) + shape,
                                         lambda i: (layer, jnp.where(i >= half, 1, 0)) + (0,) * len(shape))
    return pl.pallas_call(
        functools.partial(_filter_kernel, n=n),
        grid=(2 * half,),
        in_specs=[
            pl.BlockSpec((rows, HY_FEAT_PAD), lambda i: (i, 0)),
            pl.BlockSpec((rows, V7X_LANES), lambda i: (i, 0)),
            lay(HY_FEAT_PAD, HY_HIDDEN), lay(1, HY_HIDDEN), lay(HY_HIDDEN, HY_HIDDEN), lay(1, HY_HIDDEN),
            by_dir(HY_HIDDEN, HY_TAP_COLS), lay(1, HY_HIDDEN), by_dir(1, HY_TAP_COLS),
        ],
        out_specs=[pl.BlockSpec((rows, HY_TAP_COLS), lambda i: (i, 0)),
                   pl.BlockSpec((V7X_SUBLANES, HY_TAP_COLS), lambda i: (0, 0))],
        out_shape=[jax.ShapeDtypeStruct((2 * n, HY_TAP_COLS), _BF16),
                   jax.ShapeDtypeStruct((V7X_SUBLANES, HY_TAP_COLS), _F32)],
        compiler_params=_cparams("arbitrary"),
        name="hyena_filter_taps",
    )(feat, tn, w1, b1, w2, b2, w3_dir, freq, decay_dir)


def _dft_block(n_out, n_in, sign, scale=1.0, real_input=False):
    size = max(n_out, n_in)
    ang = 2.0 * np.pi * np.outer(np.arange(n_out), np.arange(n_in)) / size
    fr, fi = np.cos(ang) * scale, sign * np.sin(ang) * scale
    blk = np.concatenate([fr, fi], 0) if real_input else np.block([[fr, -fi], [fi, fr]])
    return jnp.asarray(blk, _F32).astype(_BF16)


def _twiddle_table(n1, n2, sign):
    ang = 2.0 * np.pi * np.outer(np.arange(n1), np.arange(n2)) / (n1 * n2)
    tw = np.stack([np.cos(ang), sign * np.sin(ang)], 0)[..., None]
    return jnp.broadcast_to(jnp.asarray(tw, _F32), (2, n1, n2, V7X_LANES))


def _cmul(ar, ai, br, bi):
    return ar * br - ai * bi, ar * bi + ai * br


def _lanes_mul(xr, xi, twr, twi):
    outs_r, outs_i = [], []
    for c in range(xr.shape[1] // V7X_LANES):
        sl = slice(c * V7X_LANES, (c + 1) * V7X_LANES)
        r, i = _cmul(xr[:, sl], xi[:, sl], twr, twi)
        outs_r.append(r)
        outs_i.append(i)
    return jnp.concatenate(outs_r, 1), jnp.concatenate(outs_i, 1)


def _dft_kernel(*refs, real_input, pre_tw, has_filter, has_second, post_tw, slabs):
    refs = list(refs)
    xr_ref = refs.pop(0)
    xi_ref = None if real_input else refs.pop(0)
    f_ref = refs.pop(0)
    tw_ref = refs.pop(0) if (pre_tw or post_tw) else None
    if has_filter:
        hr_ref, hi_ref, l1_ref = refs.pop(0), refs.pop(0), refs.pop(0)
    g_ref = refs.pop(0) if has_second else None
    or_ref, oi_ref = refs
    for a in range(slabs):
        xr = xr_ref[a].astype(_F32)
        if real_input:
            s = xr.astype(_BF16)
        else:
            xi = xi_ref[a].astype(_F32)
            if pre_tw:
                xr, xi = _lanes_mul(xr, xi, tw_ref[0, a], tw_ref[1, a])
            s = jnp.concatenate([xr, xi], 0).astype(_BF16)
        y = _dot(f_ref[...], s)
        m = y.shape[0] // 2
        yr, yi = y[:m], y[m:]
        if has_filter:
            inv = 1.0 / (jnp.sum(l1_ref[...], 0, keepdims=True) + EPS)
            yr, yi = _cmul(yr, yi, hr_ref[a].astype(_F32) * inv, hi_ref[a].astype(_F32) * inv)
        if has_second:
            z = _dot(g_ref[...], jnp.concatenate([yr, yi], 0).astype(_BF16))
            m = z.shape[0] // 2
            yr, yi = z[:m], z[m:]
        if post_tw:
            yr, yi = _lanes_mul(yr, yi, tw_ref[0, a], -tw_ref[1, a])
        or_ref[a] = yr.astype(or_ref.dtype)
        oi_ref[a] = yi.astype(oi_ref.dtype)


def dft_apply(xr, xi, f_blk, *, tw=None, pre_tw=False, post_tw=False, filt=None, g_blk=None, out_dtype=_BF16):
    n_a, k, c = xr.shape
    real_input = xi is None
    m1 = f_blk.shape[0] // 2
    m_out = g_blk.shape[0] // 2 if g_blk is not None else m1
    ct = min(HY_CT, c)
    slabs = max(1, min(n_a, 1024 // max(k, m_out)))
    slab_spec = lambda rows: pl.BlockSpec((slabs, rows, ct), lambda j, a: (a, 0, j))
    whole = lambda arr: pl.BlockSpec(arr.shape, lambda j, a: (0,) * arr.ndim)
    in_specs, args = [slab_spec(k)], [xr]
    if not real_input:
        in_specs.append(slab_spec(k))
        args.append(xi)
    in_specs.append(whole(f_blk))
    args.append(f_blk)
    if pre_tw or post_tw:
        in_specs.append(pl.BlockSpec((2, slabs, tw.shape[2], V7X_LANES), lambda j, a: (0, a, 0, 0)))
        args.append(tw)
    if filt is not None:
        h_re, h_im, l1, first_block = filt
        h_spec = pl.BlockSpec((slabs, m1, ct), lambda j, a: (a, 0, j + first_block))
        l1_spec = pl.BlockSpec((V7X_SUBLANES, ct), lambda j, a: (0, j + first_block))
        in_specs += [h_spec, h_spec, l1_spec]
        args += [h_re, h_im, l1]
    if g_blk is not None:
        in_specs.append(whole(g_blk))
        args.append(g_blk)
    kern = functools.partial(_dft_kernel, real_input=real_input, pre_tw=pre_tw, has_filter=filt is not None,
                             has_second=g_blk is not None, post_tw=post_tw, slabs=slabs)
    return pl.pallas_call(
        kern,
        grid=(c // ct, n_a // slabs),
        in_specs=in_specs,
        out_specs=[slab_spec(m_out)] * 2,
        out_shape=[jax.ShapeDtypeStruct((n_a, m_out, c), out_dtype)] * 2,
        compiler_params=_cparams("parallel", "parallel"),
        name="hyena_dft",
    )(*args)


def _fft_factors(n_fft):
    n1 = 1 << ((n_fft.bit_length() - 1) // 2)
    return n_fft // n1, n1


def filter_spectrum(taps):
    n_fft, c = taps.shape
    if n_fft <= 512:
        return dft_apply(taps[None], None, _dft_block(n_fft, n_fft, -1.0, real_input=True))
    n1, n2 = _fft_factors(n_fft)
    x = jnp.swapaxes(taps.reshape(n1, n2, c), 0, 1)
    yr, yi = dft_apply(x, None, _dft_block(n1, n1, -1.0, real_input=True))
    yr, yi = jnp.swapaxes(yr, 0, 1), jnp.swapaxes(yi, 0, 1)
    return dft_apply(yr, yi, _dft_block(n2, n2, -1.0), tw=_twiddle_table(n1, n2, -1.0), pre_tw=True)


def long_conv_pair(vr, vi, filt):
    n, c = vr.shape
    n_fft = 2 * n
    if n_fft <= 512:
        f = _dft_block(n_fft, n, -1.0)
        g = _dft_block(n, n_fft, 1.0, scale=1.0 / n_fft)
        yr, yi = dft_apply(vr[None], vi[None], f, filt=filt, g_blk=g)
        return yr[0], yi[0]
    n1, n2 = _fft_factors(n_fft)
    n1h = n1 // 2
    to_slabs = lambda a: jnp.swapaxes(a.reshape(n1h, n2, c), 0, 1)
    yr, yi = dft_apply(to_slabs(vr), to_slabs(vi), _dft_block(n1, n1h, -1.0))
    yr, yi = jnp.swapaxes(yr, 0, 1), jnp.swapaxes(yi, 0, 1)
    tw = _twiddle_table(n1, n2, -1.0)
    zr, zi = dft_apply(yr, yi, _dft_block(n2, n2, -1.0), tw=tw, pre_tw=True, post_tw=True, filt=filt,
                       g_blk=_dft_block(n2, n2, 1.0))
    zr, zi = jnp.swapaxes(zr, 0, 1), jnp.swapaxes(zi, 0, 1)
    outr, outi = dft_apply(zr, zi, _dft_block(n1h, n1, 1.0, scale=1.0 / n_fft))
    back = lambda a: jnp.swapaxes(a, 0, 1).reshape(n, c)
    return back(outr), back(outi)


def _gate_kernel(x_ref, conv_ref, v_ref, skip_ref, o_ref):
    v = v_ref[...].astype(_F32)
    o_ref[...] = (x_ref[...].astype(_F32) * (conv_ref[...].astype(_F32) + skip_ref[...] * v)).astype(o_ref.dtype)


def hyena_gate(xg, conv, v, skip, layer, order):
    bsz, n, _ = xg.shape
    rows = min(ROW_TILE, n)
    row = pl.BlockSpec((None, rows, HY_W), lambda b, i: (b, i, 0))
    return pl.pallas_call(
        _gate_kernel,
        grid=(bsz, n // rows),
        in_specs=[row, row, row, pl.BlockSpec((None, None, 1, HY_W), lambda b, i: (layer, order, 0, 0))],
        out_specs=row,
        out_shape=jax.ShapeDtypeStruct((bsz, n, HY_W), _BF16),
        compiler_params=_cparams("parallel", "parallel"),
        name="hyena_gate",
    )(xg, conv, v, skip)


def hyena_segment(v, x1, x2, w1, b1, w2, b2, w3, freq, decay, skip, layer):
    bsz, n, _ = v.shape
    assert bsz == 2, "the two batch elements are packed into one complex transform"
    taps, l1 = hyena_filter_taps(n, w1, b1, w2, b2, w3, freq, decay, layer)
    hr, hi = filter_spectrum(taps)
    u = v
    for o, xg in enumerate((x1, x2)):
        filt = (hr, hi, l1, o * (HY_W // min(HY_CT, HY_W)))
        c0, c1 = long_conv_pair(u[0], u[1], filt)
        u = hyena_gate(xg, jnp.stack([c0, c1], 0), u, skip, layer, o)
    return u


def pack_hyena_weights(p):
    depth = p["hy_w1"].shape[0]
    w1 = jnp.pad(p["hy_w1"], ((0, 0), (0, HY_FEAT_PAD - HY_EMB), (0, 0)))
    row = lambda a: a.reshape(depth, 1, -1)
    w3 = p["hy_w3"].reshape(depth, HY_HIDDEN, HY_ORDER, 2, HY_W)
    w3_dir = jnp.transpose(w3, (0, 3, 1, 2, 4)).reshape(depth, 2, HY_HIDDEN, HY_TAP_COLS).astype(_BF16)
    decay_dir = jnp.swapaxes(p["hy_decay"], 1, 2).reshape(depth, 2, 1, HY_TAP_COLS)
    return (w1, row(p["hy_b1"]), p["hy_w2"], row(p["hy_b2"]), w3_dir, row(p["hy_freq"]), decay_dir,
            p["hy_skip"].reshape(depth, HY_ORDER, 1, HY_W))


def kernel(x, c, ctx, c_ctx, w_mod, b_mod, norm1, norm2, w_in, ssm_lam_re, ssm_lam_im, ssm_log_dt, ssm_b_re, ssm_b_im, ssm_c_re, ssm_c_im, ssm_d, ssm_w_glu, ssm_b_glu, mla_q_norm, mla_w_uq, mla_kv_norm, mla_w_ukv, qk_norm_q, qk_norm_k, hy_conv_w, hy_conv_b, hy_w1, hy_b1, hy_w2, hy_b2, hy_w3, hy_freq, hy_decay, hy_skip, w_branch, w_out, router_w, router_bias, moe_w_gate, moe_w_up, moe_w_down):
    depth = w_mod.shape[0]
    bsz, n_lat, _ = x.shape
    n_ctx = ctx.shape[1]
    assert n_ctx % ROW_TILE == 0 and n_lat % ROW_TILE == 0 and ROW_TILE == ATT_TILE
    n_ctx_tiles = n_ctx // ROW_TILE
    row = lambda a: a.reshape(depth, 1, -1)

    mod_tab = modulation_table(c, c_ctx, w_mod, b_mod)
    w_in_used = pack_w_in(w_in)
    w_gates = w_in[:, :, _IN_GATES:].astype(_BF16)
    w_branch_b, w_out_b = w_branch.astype(_BF16), w_out.astype(_BF16)
    moe_b = (moe_w_gate.astype(_BF16), moe_w_up.astype(_BF16), moe_w_down.astype(_BF16))
    rw_hi, rw_lo, rb_col = split_router(router_w, router_bias)
    norm1_r, norm2_r = row(norm1), row(norm2)
    s5_consts, s5_bd, s5_cd = pack_s5(*s5_discretize(ssm_lam_re, ssm_lam_im, ssm_log_dt, ssm_b_re, ssm_b_im),
                                      ssm_c_re, ssm_c_im)
    s5_finish = (row(ssm_d), ssm_w_glu.astype(_BF16), row(ssm_b_glu))
    mla_packed = pack_mla_weights(mla_w_uq, mla_w_ukv, qk_norm_q, qk_norm_k)
    mla_tables = rope_tables(n_ctx, n_lat)
    att_bounds = score_bound(qk_norm_q, qk_norm_k)
    hy_w = pack_hyena_weights(dict(hy_w1=hy_w1, hy_b1=hy_b1, hy_w2=hy_w2, hy_b2=hy_b2, hy_w3=hy_w3, hy_freq=hy_freq,
                                   hy_decay=hy_decay, hy_skip=hy_skip))
    hy_conv_b_r = row(hy_conv_b)

    xcat = jnp.concatenate([ctx, x], 1)
    for l in range(depth):
        u, q_lat, kv_lat, hy, k_rope = input_projection(xcat, mod_tab, norm1_r, w_in_used, l, n_ctx_tiles)

        y_fwd = s5_scan(u, s5_consts, s5_bd, s5_cd, l, n_ctx_tiles, False)
        y_ssm = s5_scan(u, s5_consts, s5_bd, s5_cd, l, n_ctx_tiles, True, (y_fwd,) + s5_finish)

        q_t, k, v_t = mla_prepare(q_lat, kv_lat, k_rope, row(mla_q_norm), row(mla_kv_norm), mla_packed, mla_tables, l)
        att_ctx, att_lat = (jnp.swapaxes(a, 1, 2)
                            for a in flash_attention(q_t, k, v_t, n_ctx_tiles, att_bounds[l:l + 1]))

        v_c, x1_c, x2_c, v_l, x1_l, x2_l = hyena_short_conv(hy, hy_conv_w, hy_conv_b_r, l, n_ctx)
        hy_lat = hyena_segment(v_l, x1_l, x2_l, *hy_w, l)
        if l < depth - 1:
            hy_ctx = hyena_segment(v_c, x1_c, x2_c, *hy_w, l)
        else:
            hy_ctx = jnp.zeros((bsz, n_ctx, HY_W), hy_lat.dtype)

        x_new, hl, route, counts = merge_and_route(xcat, y_ssm, att_ctx, att_lat, hy_ctx, hy_lat, mod_tab, norm1_r,
                                                   norm2_r, w_gates, w_branch_b, w_out_b, rw_hi, rw_lo, rb_col, l,
                                                   n_ctx_tiles)
        xcat = moe_block(x_new, hl, route, counts, mod_tab, *moe_b, l, n_ctx_tiles)
    return xcat[:, n_ctx:]
```

```python
import functools
import math

import numpy as np
import jax
import jax.numpy as jnp
from jax import lax
from jax.experimental import pallas as pl
from jax.experimental.pallas import tpu as pltpu

D_MODEL = 1024
EPS = 1e-6
GRID_W = 64
SSM_W = 512
SSM_GROUP = 16
SSM_GROUPS = SSM_W // SSM_GROUP
SSM_STATE = 64
MLA_HEADS = 8
MLA_Q_RANK = 384
MLA_KV_RANK = 256
MLA_NOPE = 64
MLA_ROPE = 32
MLA_V = 64
MLA_QK = MLA_NOPE + MLA_ROPE
ROPE_THETA = 10000.0
HY_W = 512
HY_ORDER = 2
HY_BANDS = 16
HY_EMB = 1 + 2 * HY_BANDS
HY_HIDDEN = 64
N_BRANCH = 3
IN_SPLITS = (SSM_W, MLA_Q_RANK, MLA_KV_RANK, MLA_ROPE, 3 * HY_W)
N_EXPERTS = 16
N_EXPERT_GROUPS = 4
EXPERTS_PER_GROUP = N_EXPERTS // N_EXPERT_GROUPS
D_EXPERT = 512

V7X_LANES = 128
V7X_SUBLANES = 8
V7X_VMEM_LIMIT = 56 * 1024 * 1024

ROW_TILE = 256
MOE_TILE = 256
HEAD_PAD = 128

_F32 = jnp.float32
_BF16 = jnp.bfloat16


def _cparams(*sem):
    return pltpu.CompilerParams(dimension_semantics=sem, vmem_limit_bytes=V7X_VMEM_LIMIT)


def _silu(v):
    return v * jax.nn.sigmoid(v)


def _modulated_norm(x, g, shift, scale):
    y = x * lax.rsqrt(jnp.mean(x * x, -1, keepdims=True) + EPS)
    return (y * g) * (1.0 + scale) + shift


def _dot(a, b):
    return jnp.dot(a, b, preferred_element_type=_F32)


def _mod_kernel(c_ref, w_ref, b_ref, o_ref):
    s = _silu(c_ref[...])
    o_ref[...] = jnp.dot(s, w_ref[...], preferred_element_type=_F32, precision=lax.Precision.HIGHEST) + b_ref[...]


def modulation_table(c, c_ctx, w_mod, b_mod):
    depth = w_mod.shape[0]
    bsz = c.shape[0]
    rows = V7X_SUBLANES * pl.cdiv(bsz + 1, V7X_SUBLANES)
    cvec = jnp.concatenate([c, c_ctx[None, :], jnp.zeros((rows - bsz - 1, D_MODEL), _F32)], 0)
    out = pl.pallas_call(
        _mod_kernel,
        grid=(depth, 6),
        in_specs=[
            pl.BlockSpec((rows, D_MODEL), lambda l, j: (0, 0)),
            pl.BlockSpec((None, D_MODEL, D_MODEL), lambda l, j: (l, 0, j)),
            pl.BlockSpec((None, 1, D_MODEL), lambda l, j: (l, 0, j)),
        ],
        out_specs=pl.BlockSpec((None, rows, D_MODEL), lambda l, j: (l, 0, j)),
        out_shape=jax.ShapeDtypeStruct((depth, rows, 6 * D_MODEL), _F32),
        compiler_params=_cparams("parallel", "parallel"),
        name="modulation",
    )(cvec, w_mod, b_mod.reshape(depth, 1, 6 * D_MODEL))
    m = out.reshape(depth, rows, 6, D_MODEL)
    lat = m[:, :bsz]
    ctx = jnp.broadcast_to(m[:, bsz:bsz + 1], lat.shape)
    tab = jnp.stack([ctx, lat], 2)
    return jnp.pad(tab, ((0, 0), (0, 0), (0, 0), (0, V7X_SUBLANES - 6), (0, 0)))


def _mod_spec(layer, n_ctx_tiles):
    return pl.BlockSpec((None, None, None, V7X_SUBLANES, D_MODEL),
                        lambda b, i: (layer, b, jnp.where(i >= n_ctx_tiles, 1, 0), 0, 0))


_IN_OFF = (0, SSM_W, SSM_W + MLA_Q_RANK, SSM_W + MLA_Q_RANK + MLA_KV_RANK)
_IN_HY = _IN_OFF[3]
_IN_ROPE = _IN_HY + 3 * HY_W
_IN_USED = _IN_ROPE + MLA_ROPE
_IN_GATES = sum(IN_SPLITS)


def pack_w_in(w_in):
    a = SSM_W + MLA_Q_RANK + MLA_KV_RANK
    return jnp.concatenate([w_in[:, :, :a], w_in[:, :, a + MLA_ROPE:_IN_GATES], w_in[:, :, a:a + MLA_ROPE]],
                           -1).astype(_BF16)


def split_router(router_w, router_bias):
    rw = router_w.T
    hi = rw.astype(_BF16)
    lo = (rw - hi.astype(_F32)).astype(_BF16)
    return hi, lo, router_bias.reshape(N_EXPERTS, 1).astype(_F32)


def _inproj_kernel(x_ref, mod_ref, g_ref, w_ref, u_ref, q_ref, kv_ref, hy_ref, kr_ref):
    xn = _modulated_norm(x_ref[...], g_ref[...], mod_ref[0:1, :], mod_ref[1:2, :]).astype(_BF16)
    u_ref[...] = _dot(xn, w_ref[:, _IN_OFF[0]:_IN_OFF[1]]).astype(u_ref.dtype)
    q_ref[...] = _dot(xn, w_ref[:, _IN_OFF[1]:_IN_OFF[2]]).astype(q_ref.dtype)
    kv_ref[...] = _dot(xn, w_ref[:, _IN_OFF[2]:_IN_OFF[3]]).astype(kv_ref.dtype)
    hy_ref[...] = _dot(xn, w_ref[:, _IN_HY:_IN_ROPE]).astype(hy_ref.dtype)
    kr_ref[...] = _dot(xn, w_ref[:, _IN_ROPE:_IN_USED]).astype(kr_ref.dtype)


def input_projection(x, mod_tab, norm1, w_in_used, layer, n_ctx_tiles):
    bsz, t, _ = x.shape
    widths = (SSM_W, MLA_Q_RANK, MLA_KV_RANK, 3 * HY_W, MLA_ROPE)
    row = lambda w: pl.BlockSpec((None, ROW_TILE, w), lambda b, i: (b, i, 0))
    return pl.pallas_call(
        _inproj_kernel,
        grid=(bsz, t // ROW_TILE),
        in_specs=[
            row(D_MODEL),
            _mod_spec(layer, n_ctx_tiles),
            pl.BlockSpec((None, 1, D_MODEL), lambda b, i: (layer, 0, 0)),
            pl.BlockSpec((None, D_MODEL, _IN_USED), lambda b, i: (layer, 0, 0)),
        ],
        out_specs=[row(w) for w in widths],
        out_shape=[jax.ShapeDtypeStruct((bsz, t, w), _BF16) for w in widths],
        compiler_params=_cparams("parallel", "parallel"),
        name="input_projection",
    )(x, mod_tab, norm1, w_in_used)


def _max4(a, b, c, d):
    hi1, lo1 = jnp.maximum(a, b), jnp.minimum(a, b)
    hi2, lo2 = jnp.maximum(c, d), jnp.minimum(c, d)
    return jnp.maximum(hi1, hi2), jnp.maximum(jnp.minimum(hi1, hi2), jnp.maximum(lo1, lo2))


def _first_argmax(vals):
    best, idx = vals[0], jnp.zeros(vals[0].shape, jnp.int32)
    for j in range(1, len(vals)):
        upd = vals[j] > best
        idx = jnp.where(upd, j, idx)
        best = jnp.where(upd, vals[j], best)
    return idx


def _pick(vals, idx):
    out = vals[-1]
    for j in range(len(vals) - 2, -1, -1):
        out = jnp.where(idx == j, vals[j], out)
    return out


def _route_rows(logits_t, bias_col):
    s = jax.nn.sigmoid(logits_t)
    sel = s + bias_col
    s_rows = [s[e:e + 1, :] for e in range(N_EXPERTS)]
    sel_rows = [sel[e:e + 1, :] for e in range(N_EXPERTS)]
    gsum = []
    for g in range(N_EXPERT_GROUPS):
        top, second = _max4(*sel_rows[EXPERTS_PER_GROUP * g:EXPERTS_PER_GROUP * (g + 1)])
        gsum.append(top + second)
    gidx = _first_argmax(gsum)
    v = [_pick([sel_rows[EXPERTS_PER_GROUP * g + j] for g in range(N_EXPERT_GROUPS)], gidx)
         for j in range(EXPERTS_PER_GROUP)]
    u = [_pick([s_rows[EXPERTS_PER_GROUP * g + j] for g in range(N_EXPERT_GROUPS)], gidx)
         for j in range(EXPERTS_PER_GROUP)]
    i1 = _first_argmax(v)
    neg = jnp.full(v[0].shape, -jnp.inf, _F32)
    i2 = _first_argmax([jnp.where(i1 == j, neg, v[j]) for j in range(EXPERTS_PER_GROUP)])
    wa, wb = _pick(u, i1), _pick(u, i2)
    tot = wa + wb
    return EXPERTS_PER_GROUP * gidx + i1, EXPERTS_PER_GROUP * gidx + i2, wa / tot, wb / tot


def _expert_ranks(e0, e1, tri_ref, run_ref):
    n = e0.shape[1]
    rows = lax.broadcasted_iota(jnp.int32, (N_EXPERTS, n), 0)
    oh0, oh1 = rows == e0, rows == e1
    c = jnp.where(oh0, 1.0, 0.0) + jnp.where(oh1, 1.0, 0.0)
    before = _dot(c.astype(_BF16), tri_ref[...])
    run = run_ref[...]
    base = before + jnp.concatenate([run] * (n // V7X_LANES), 1)
    rank0 = jnp.sum(jnp.where(oh0, base, 0.0), 0, keepdims=True)
    rank1 = jnp.sum(jnp.where(oh1, base, 0.0), 0, keepdims=True)
    run_ref[...] = run + jnp.broadcast_to(jnp.sum(c, 1, keepdims=True), run.shape)
    return rank0, rank1


def _merge_kernel(x_ref, ys_ref, yac_ref, yal_ref, yhc_ref, yhl_ref, mod_ref, g1_ref, g2_ref, wg_ref, wb_ref, wo_ref,
                  rwh_ref, rwl_ref, rb_ref, tri_ref, xo_ref, hl_ref, rt_ref, cnt_ref, run_ref, *, n_ctx_tiles):
    x = x_ref[...]
    xn = _modulated_norm(x, g1_ref[...], mod_ref[0:1, :], mod_ref[1:2, :]).astype(_BF16)
    in_ctx = pl.program_id(1) < n_ctx_tiles
    y_att = jnp.where(in_ctx, yac_ref[...], yal_ref[...])
    y_hy = jnp.where(in_ctx, yhc_ref[...], yhl_ref[...])
    m = None
    for k, y in enumerate((ys_ref[...], y_att, y_hy)):
        gate = jax.nn.sigmoid(_dot(xn, wg_ref[:, k * D_MODEL:(k + 1) * D_MODEL]))
        term = gate * _dot(y, wb_ref[k])
        m = term if m is None else m + term
    x_new = x + mod_ref[2:3, :] * _dot(m.astype(_BF16), wo_ref[...])
    xo_ref[...] = x_new
    hl = _modulated_norm(x_new, g2_ref[...], mod_ref[3:4, :], mod_ref[4:5, :])
    hl_ref[...] = hl
    hl_hi = hl.astype(_BF16)
    hl_lo = (hl - hl_hi.astype(_F32)).astype(_BF16)
    nt = (((1,), (1,)), ((), ()))
    logits_t = (lax.dot_general(rwh_ref[...], hl_hi, nt, preferred_element_type=_F32)
                + lax.dot_general(rwl_ref[...], hl_hi, nt, preferred_element_type=_F32)
                + lax.dot_general(rwh_ref[...], hl_lo, nt, preferred_element_type=_F32))
    e0, e1, w0, w1 = _route_rows(logits_t, rb_ref[...])

    @pl.when((pl.program_id(0) == 0) & (pl.program_id(1) == 0))
    def _():
        run_ref[...] = jnp.zeros_like(run_ref)

    rank0, rank1 = _expert_ranks(e0, e1, tri_ref, run_ref)
    cnt_ref[...] = run_ref[...]
    zero = jnp.zeros_like(w0)
    rt_ref[...] = jnp.concatenate([e0.astype(_F32), e1.astype(_F32), w0, w1, rank0, rank1, zero, zero], 0)


def merge_and_route(x, y_ssm, att_ctx, att_lat, hy_ctx, hy_lat, mod_tab, norm1, norm2, w_gates, w_branch, w_out,
                    rw_hi, rw_lo, rb_col, layer, n_ctx_tiles):
    bsz, t, _ = x.shape
    row = lambda w: pl.BlockSpec((None, ROW_TILE, w), lambda b, i: (b, i, 0))
    whole = lambda *shape: pl.BlockSpec(shape, lambda b, i: (0,) * len(shape))
    ctx_row = lambda w: pl.BlockSpec((None, ROW_TILE, w), lambda b, i: (b, jnp.minimum(i, n_ctx_tiles - 1), 0))
    lat_row = lambda w: pl.BlockSpec((None, ROW_TILE, w), lambda b, i: (b, jnp.maximum(i - n_ctx_tiles, 0), 0))
    return pl.pallas_call(
        functools.partial(_merge_kernel, n_ctx_tiles=n_ctx_tiles),
        grid=(bsz, t // ROW_TILE),
        in_specs=[
            row(D_MODEL), row(SSM_W),
            ctx_row(MLA_HEADS * MLA_V), lat_row(MLA_HEADS * MLA_V), ctx_row(HY_W), lat_row(HY_W),
            _mod_spec(layer, n_ctx_tiles),
            pl.BlockSpec((None, 1, D_MODEL), lambda b, i: (layer, 0, 0)),
            pl.BlockSpec((None, 1, D_MODEL), lambda b, i: (layer, 0, 0)),
            pl.BlockSpec((None, D_MODEL, N_BRANCH * D_MODEL), lambda b, i: (layer, 0, 0)),
            pl.BlockSpec((None, N_BRANCH, SSM_W, D_MODEL), lambda b, i: (layer, 0, 0, 0)),
            pl.BlockSpec((None, D_MODEL, D_MODEL), lambda b, i: (layer, 0, 0)),
            whole(N_EXPERTS, D_MODEL), whole(N_EXPERTS, D_MODEL), whole(N_EXPERTS, 1),
            whole(ROW_TILE, ROW_TILE),
        ],
        out_specs=[row(D_MODEL), row(D_MODEL),
                   pl.BlockSpec((None, V7X_SUBLANES, ROW_TILE), lambda b, i: (b, 0, i)),
                   whole(N_EXPERTS, V7X_LANES)],
        out_shape=[jax.ShapeDtypeStruct((bsz, t, D_MODEL), _F32),
                   jax.ShapeDtypeStruct((bsz, t, D_MODEL), _F32),
                   jax.ShapeDtypeStruct((bsz, V7X_SUBLANES, t), _F32),
                   jax.ShapeDtypeStruct((N_EXPERTS, V7X_LANES), _F32)],
        scratch_shapes=[pltpu.VMEM((N_EXPERTS, V7X_LANES), _F32)],
        compiler_params=_cparams("arbitrary", "arbitrary"),
        name="merge_and_route",
    )(x, y_ssm, att_ctx, att_lat, hy_ctx, hy_lat, mod_tab, norm1, norm2, w_gates, w_branch, w_out, rw_hi, rw_lo,
      rb_col, jnp.triu(jnp.ones((ROW_TILE, ROW_TILE), _BF16), 1))


def _moe_kernel(te_ref, tv_ref, xs_ref, wg_ref, wu_ref, wd_ref, o_ref):
    del te_ref

    @pl.when(tv_ref[pl.program_id(0)] > 0)
    def _():
        h = xs_ref[...].astype(_BF16)
        act = (_silu(_dot(h, wg_ref[...])) * _dot(h, wu_ref[...])).astype(_BF16)
        o_ref[...] = _dot(act, wd_ref[...]).astype(o_ref.dtype)

    @pl.when(tv_ref[pl.program_id(0)] == 0)
    def _():
        o_ref[...] = jnp.zeros_like(o_ref)


def grouped_experts(xs, tile_expert, tile_valid, w_gate, w_up, w_down, layer):
    rows = xs.shape[0]
    n_tiles = rows // MOE_TILE
    wspec = lambda a, b: pl.BlockSpec((None, None, a, b), lambda i, te, tv: (layer, te[i], 0, 0))
    return pl.pallas_call(
        _moe_kernel,
        grid_spec=pltpu.PrefetchScalarGridSpec(
            num_scalar_prefetch=2,
            grid=(n_tiles,),
            in_specs=[
                pl.BlockSpec((MOE_TILE, D_MODEL), lambda i, te, tv: (i, 0)),
                wspec(D_MODEL, D_EXPERT), wspec(D_MODEL, D_EXPERT), wspec(D_EXPERT, D_MODEL),
            ],
            out_specs=pl.BlockSpec((MOE_TILE, D_MODEL), lambda i, te, tv: (i, 0)),
        ),
        out_shape=jax.ShapeDtypeStruct((rows, D_MODEL), _BF16),
        compiler_params=_cparams("arbitrary"),
        name="grouped_experts",
    )(tile_expert, tile_valid, xs, w_gate, w_up, w_down)


def expert_dispatch_plan(route, counts):
    bsz, _, t = route.shape
    n = bsz * t
    slots = lambda r0: jnp.stack([route[:, r0, :], route[:, r0 + 1, :]], -1).reshape(n * 2).astype(jnp.int32)
    e, rank = slots(0), slots(4)
    count = counts[:, 0].astype(jnp.int32)
    tiles = (count + MOE_TILE - 1) // MOE_TILE
    tile_start = jnp.cumsum(tiles) - tiles
    dest = tile_start[e] * MOE_TILE + rank
    rows = 2 * n + N_EXPERTS * MOE_TILE
    n_tiles = rows // MOE_TILE
    tile_ids = jnp.arange(n_tiles, dtype=jnp.int32)
    ends = jnp.cumsum(tiles)
    tile_expert = jnp.minimum(jnp.sum(tile_ids[:, None] >= ends[None, :], 1), N_EXPERTS - 1)
    tile_valid = (tile_ids < jnp.sum(tiles)).astype(jnp.int32)
    src = jnp.zeros((rows,), jnp.int32).at[dest].set(jnp.arange(2 * n, dtype=jnp.int32) // 2)
    dest2 = dest.reshape(n, 2)
    return src, tile_expert.astype(jnp.int32), tile_valid, dest2[:, 0], dest2[:, 1]


def _combine_kernel(x_ref, a_ref, b_ref, wa_ref, wb_ref, mod_ref, o_ref):
    wa, wb = wa_ref[...], wb_ref[...]
    for j in range(D_MODEL // V7X_LANES):
        sl = slice(j * V7X_LANES, (j + 1) * V7X_LANES)
        y = wa * a_ref[:, sl].astype(_F32) + wb * b_ref[:, sl].astype(_F32)
        o_ref[:, sl] = x_ref[:, sl] + mod_ref[5:6, sl] * y


def combine_experts(x, ya, yb, wa, wb, mod_tab, layer, n_ctx_tiles):
    bsz, t, _ = x.shape
    row = lambda w: pl.BlockSpec((None, ROW_TILE, w), lambda b, i: (b, i, 0))
    return pl.pallas_call(
        _combine_kernel,
        grid=(bsz, t // ROW_TILE),
        in_specs=[row(D_MODEL), row(D_MODEL), row(D_MODEL), row(V7X_LANES), row(V7X_LANES),
                  _mod_spec(layer, n_ctx_tiles)],
        out_specs=row(D_MODEL),
        out_shape=jax.ShapeDtypeStruct(x.shape, _F32),
        compiler_params=_cparams("parallel", "parallel"),
        name="combine_experts",
    )(x, ya, yb, wa, wb, mod_tab)


def moe_block(x_new, hl, route, counts, mod_tab, w_gate, w_up, w_down, layer, n_ctx_tiles):
    bsz, t, _ = x_new.shape
    src, tile_expert, tile_valid, d0, d1 = expert_dispatch_plan(route, counts)
    xs = jnp.take(hl.reshape(bsz * t, D_MODEL), src, axis=0, mode="clip")
    ys = grouped_experts(xs, tile_expert, tile_valid, w_gate, w_up, w_down, layer)
    ya = jnp.take(ys, d0, axis=0, mode="clip").reshape(bsz, t, D_MODEL)
    yb = jnp.take(ys, d1, axis=0, mode="clip").reshape(bsz, t, D_MODEL)
    lanes = lambda r: jnp.broadcast_to(route[:, r, :, None], (bsz, t, V7X_LANES))
    return combine_experts(x_new, ya, yb, lanes(2), lanes(3), mod_tab, layer, n_ctx_tiles)


ATT_TILE = 256
ATT_CHUNK_TILES = 2
ATT_UNROLL = 16
ATT_LATENT_TQ = 512
ATT_MAX_BOUND = 60.0
V_ROWS = MLA_V + 16
Q_SCALE = MLA_QK ** -0.5 * math.log2(math.e)
_ROPE_LO = MLA_NOPE
_ROPE_HALF = MLA_ROPE // 2


def rope_tables(n_ctx, n_lat):
    rows = n_lat // GRID_W
    row = jnp.broadcast_to(jnp.arange(rows, dtype=_F32)[:, None], (rows, GRID_W)).reshape(n_lat)
    col = jnp.broadcast_to(jnp.arange(GRID_W, dtype=_F32)[None, :], (rows, GRID_W)).reshape(n_lat)
    n_f = MLA_ROPE // 4
    inv = ROPE_THETA ** (-jnp.arange(n_f, dtype=_F32) / n_f)
    ang = jnp.concatenate([row[:, None] * inv, col[:, None] * inv], -1)
    cos = jnp.concatenate([jnp.ones((n_ctx, _ROPE_HALF), _F32), jnp.cos(ang)], 0)
    sin = jnp.concatenate([jnp.zeros((n_ctx, _ROPE_HALF), _F32), jnp.sin(ang)], 0)
    t = n_ctx + n_lat
    ones, zeros = jnp.ones((t, MLA_NOPE), _F32), jnp.zeros((t, MLA_NOPE), _F32)
    tail1, tail0 = jnp.ones((t, HEAD_PAD - MLA_QK), _F32), jnp.zeros((t, HEAD_PAD - MLA_QK), _F32)
    cos_p = jnp.concatenate([ones, cos, cos, tail1], -1)
    sin_p = jnp.concatenate([zeros, -sin, sin, tail0], -1)
    return cos.T, sin.T, cos_p, sin_p


def pack_mla_weights(w_uq, w_ukv, qk_norm_q, qk_norm_k):
    depth = w_uq.shape[0]
    pad = HEAD_PAD - MLA_QK
    wq = w_uq.reshape(depth, MLA_Q_RANK, MLA_HEADS, MLA_QK)
    wq = jnp.pad(wq, ((0, 0), (0, 0), (0, 0), (0, pad))).reshape(depth, MLA_Q_RANK, MLA_HEADS * HEAD_PAD)
    wq_t = jnp.swapaxes(wq, 1, 2).astype(_BF16)
    wkv = w_ukv.reshape(depth, MLA_KV_RANK, MLA_HEADS, MLA_NOPE + MLA_V)
    wk = jnp.pad(wkv[..., :MLA_NOPE], ((0, 0), (0, 0), (0, 0), (0, HEAD_PAD - MLA_NOPE)))
    wk = wk.reshape(depth, MLA_KV_RANK, MLA_HEADS * HEAD_PAD).astype(_BF16)
    wv_t = jnp.swapaxes(wkv[..., MLA_NOPE:].reshape(depth, MLA_KV_RANK, MLA_HEADS * MLA_V), 1, 2).astype(_BF16)
    place = jnp.zeros((MLA_ROPE, HEAD_PAD), _F32).at[jnp.arange(MLA_ROPE), _ROPE_LO + jnp.arange(MLA_ROPE)].set(1.0)
    place = jnp.tile(place, (1, MLA_HEADS)).astype(_BF16)
    gq = jnp.pad(qk_norm_q, ((0, 0), (0, pad)))
    gq_col = jnp.broadcast_to(gq[:, :, None], (depth, HEAD_PAD, ATT_TILE)).astype(_F32)
    gk_row = jnp.pad(qk_norm_k, ((0, 0), (0, pad))).reshape(depth, 1, HEAD_PAD).astype(_F32)
    return wq_t, wk, wv_t, place, gq_col, gk_row


def _mla_prep_kernel(ql_ref, kvl_ref, kr_ref, qn_ref, kvn_ref, wq_ref, wk_ref, wv_ref, place_ref, gq_ref, gk_ref,
                     cos_t_ref, sin_t_ref, cos_p_ref, sin_p_ref, qt_ref, k_ref, vt_ref):
    nt = (((1,), (1,)), ((), ()))
    ql = ql_ref[...].astype(_F32)
    qn = (ql * lax.rsqrt(jnp.mean(ql * ql, -1, keepdims=True) + EPS) * qn_ref[...]).astype(_BF16)
    kvl = kvl_ref[...].astype(_F32)
    kvn = (kvl * lax.rsqrt(jnp.mean(kvl * kvl, -1, keepdims=True) + EPS) * kvn_ref[...]).astype(_BF16)

    q_t = lax.dot_general(wq_ref[...], qn, nt, preferred_element_type=_F32)
    cos_t, sin_t = cos_t_ref[...], sin_t_ref[...]
    scale = Q_SCALE
    lo, mid, hi = _ROPE_LO, _ROPE_LO + _ROPE_HALF, _ROPE_LO + MLA_ROPE
    for h in range(MLA_HEADS):
        blk = q_t[h * HEAD_PAD:(h + 1) * HEAD_PAD, :]
        ms = jnp.sum(blk * blk, 0, keepdims=True) * (1.0 / MLA_QK)
        y = blk * (lax.rsqrt(ms + EPS) * scale) * gq_ref[...]
        r1, r2 = y[lo:mid, :], y[mid:hi, :]
        rot = jnp.concatenate([y[:lo, :], r1 * cos_t - r2 * sin_t, r2 * cos_t + r1 * sin_t, y[hi:, :]], 0)
        qt_ref[h] = rot.astype(qt_ref.dtype)

    k_pre = _dot(kvn, wk_ref[...]) + _dot(kr_ref[...], place_ref[...])
    cos_p, sin_p = cos_p_ref[...], sin_p_ref[...]
    lane = lax.broadcasted_iota(jnp.int32, cos_p.shape, 1)
    for h in range(MLA_HEADS):
        blk = k_pre[:, h * HEAD_PAD:(h + 1) * HEAD_PAD]
        ms = jnp.sum(blk * blk, -1, keepdims=True) * (1.0 / MLA_QK)
        y = blk * lax.rsqrt(ms + EPS) * gk_ref[...]
        swap = jnp.where(lane < mid, pltpu.roll(y, HEAD_PAD - _ROPE_HALF, 1), pltpu.roll(y, _ROPE_HALF, 1))
        k_ref[h] = (y * cos_p + swap * sin_p).astype(k_ref.dtype)

    v_t = lax.dot_general(wv_ref[...], kvn, nt, preferred_element_type=_F32)
    ones = jnp.ones((V_ROWS - MLA_V, v_t.shape[1]), _F32)
    for h in range(MLA_HEADS):
        vt_ref[h] = jnp.concatenate([v_t[h * MLA_V:(h + 1) * MLA_V, :], ones], 0).astype(vt_ref.dtype)


def mla_prepare(q_lat, kv_lat, k_rope, q_norm, kv_norm, packed, tables, layer):
    bsz, t, _ = q_lat.shape
    wq_t, wk, wv_t, place, gq_col, gk_row = packed
    cos_t, sin_t, cos_p, sin_p = tables
    n_tiles = t // ATT_TILE
    row = lambda w: pl.BlockSpec((None, ATT_TILE, w), lambda b, i: (b, i, 0))
    lay = lambda *shape: pl.BlockSpec((None,) + shape, lambda b, i: (layer,) + (0,) * len(shape))
    return pl.pallas_call(
        _mla_prep_kernel,
        grid=(bsz, n_tiles),
        in_specs=[
            row(MLA_Q_RANK), row(MLA_KV_RANK), row(MLA_ROPE),
            lay(1, MLA_Q_RANK), lay(1, MLA_KV_RANK),
            lay(MLA_HEADS * HEAD_PAD, MLA_Q_RANK), lay(MLA_KV_RANK, MLA_HEADS * HEAD_PAD),
            lay(MLA_HEADS * MLA_V, MLA_KV_RANK),
            pl.BlockSpec((MLA_ROPE, MLA_HEADS * HEAD_PAD), lambda b, i: (0, 0)),
            lay(HEAD_PAD, ATT_TILE), lay(1, HEAD_PAD),
            pl.BlockSpec((_ROPE_HALF, ATT_TILE), lambda b, i: (0, i)),
            pl.BlockSpec((_ROPE_HALF, ATT_TILE), lambda b, i: (0, i)),
            pl.BlockSpec((ATT_TILE, HEAD_PAD), lambda b, i: (i, 0)),
            pl.BlockSpec((ATT_TILE, HEAD_PAD), lambda b, i: (i, 0)),
        ],
        out_specs=[
            pl.BlockSpec((None, MLA_HEADS, HEAD_PAD, ATT_TILE), lambda b, i: (b, 0, 0, i)),
            pl.BlockSpec((None, MLA_HEADS, ATT_TILE, HEAD_PAD), lambda b, i: (b, 0, i, 0)),
            pl.BlockSpec((None, MLA_HEADS, None, V_ROWS, ATT_TILE), lambda b, i: (b, 0, i, 0, 0)),
        ],
        out_shape=[
            jax.ShapeDtypeStruct((bsz, MLA_HEADS, HEAD_PAD, t), _BF16),
            jax.ShapeDtypeStruct((bsz, MLA_HEADS, t, HEAD_PAD), _BF16),
            jax.ShapeDtypeStruct((bsz, MLA_HEADS, n_tiles, V_ROWS, ATT_TILE), _BF16),
        ],
        compiler_params=_cparams("parallel", "parallel"),
        name="mla_prepare",
    )(q_lat, kv_lat, k_rope, q_norm, kv_norm, wq_t, wk, wv_t, place, gq_col, gk_row, cos_t, sin_t, cos_p, sin_p)


def _attention_scores(s_ref, slot, q_t, k_ref, tile0, n_sub):
    for j in range(n_sub):
        row0 = (tile0 + j) * ATT_TILE
        if not isinstance(row0, int):
            row0 = pl.multiple_of(row0, ATT_TILE)
        s_ref[slot, j * ATT_TILE:(j + 1) * ATT_TILE, :] = _dot(k_ref[pl.ds(row0, ATT_TILE), :], q_t)


def _attention_update(s_ref, slot, vt_ref, tile0, n_sub, carry, shift):
    m, acc = carry
    tiles = [s_ref[slot, j * ATT_TILE:(j + 1) * ATT_TILE, :] for j in range(n_sub)]
    if shift is None:
        m_new = m
        for s_j in tiles:
            m_new = jnp.maximum(m_new, jnp.max(s_j, 0, keepdims=True))
        acc = jnp.exp2(m - m_new) * acc
    else:
        m_new = m
    for j, s_j in enumerate(tiles):
        if shift is None:
            p = jnp.exp2((s_j - m_new).astype(_BF16))
        else:
            p = jnp.exp2(s_j - shift).astype(_BF16)
        acc = acc + _dot(vt_ref[tile0 + j], p)
    return m_new, acc


def _attention_kernel(bound_ref, qt_ref, k_ref, vt_ref, o_ref, s_ref, *, n_ctx_tiles, n_chunks, n_sub, unroll):
    q_t = qt_ref[...]
    tq = q_t.shape[1]
    first = lambda c: n_ctx_tiles + c * n_sub
    bound = bound_ref[0]

    def streamed(shift):
        carry = (jnp.full((1, tq), -jnp.inf, _F32), jnp.zeros((V_ROWS, tq), _F32))
        _attention_scores(s_ref, 1, q_t, k_ref, 0, n_ctx_tiles)
        carry = _attention_update(s_ref, 1, vt_ref, 0, n_ctx_tiles, carry, shift)

        def group(base, cr):
            _attention_scores(s_ref, 0, q_t, k_ref, first(base), n_sub)
            for c in range(unroll):
                if c + 1 < unroll:
                    _attention_scores(s_ref, (c + 1) % 2, q_t, k_ref, first(base + c + 1), n_sub)
                cr = _attention_update(s_ref, c % 2, vt_ref, first(base + c), n_sub, cr, shift)
            return cr

        if n_chunks == unroll:
            carry = group(0, carry)
        elif n_chunks:
            carry = lax.fori_loop(0, n_chunks // unroll, lambda g, c: group(g * unroll, c), carry)
        return carry[1]

    acc = lax.cond(bound <= ATT_MAX_BOUND, lambda _: streamed(bound), lambda _: streamed(None), 0)
    o_ref[...] = (acc[:MLA_V] / acc[MLA_V:MLA_V + 1]).astype(o_ref.dtype)


def _attention_call(bound, q_t, k, v_t, n_ctx_tiles, n_chunks, n_sub, unroll, tq, name):
    bsz, heads, _, n_q = q_t.shape
    t = k.shape[2]
    n_tiles = t // ATT_TILE
    return pl.pallas_call(
        functools.partial(_attention_kernel, n_ctx_tiles=n_ctx_tiles, n_chunks=n_chunks, n_sub=n_sub, unroll=unroll),
        grid=(bsz, heads, n_q // tq),
        in_specs=[
            pl.BlockSpec(memory_space=pltpu.SMEM),
            pl.BlockSpec((None, None, HEAD_PAD, tq), lambda b, h, i: (b, h, 0, i)),
            pl.BlockSpec((None, None, t, HEAD_PAD), lambda b, h, i: (b, h, 0, 0)),
            pl.BlockSpec((None, None, n_tiles, V_ROWS, ATT_TILE), lambda b, h, i: (b, h, 0, 0, 0)),
        ],
        out_specs=pl.BlockSpec((None, MLA_V, tq), lambda b, h, i: (b, h, i)),
        out_shape=jax.ShapeDtypeStruct((bsz, heads * MLA_V, n_q), _BF16),
        scratch_shapes=[pltpu.VMEM((2, max(n_sub, n_ctx_tiles) * ATT_TILE, tq), _F32)],
        compiler_params=_cparams("parallel", "parallel", "arbitrary"),
        name=name,
    )(bound, q_t, k, v_t)


def score_bound(qk_norm_q, qk_norm_k):
    gq = jnp.max(jnp.abs(qk_norm_q), -1)
    gk = jnp.max(jnp.abs(qk_norm_k), -1)
    return (MLA_QK * Q_SCALE * 1.02) * gq * gk


def flash_attention(q_t, k, v_t, n_ctx_tiles, bound):
    t = q_t.shape[3]
    n_ctx = n_ctx_tiles * ATT_TILE
    n_lat_tiles = t // ATT_TILE - n_ctx_tiles
    n_sub = math.gcd(n_lat_tiles, ATT_CHUNK_TILES)
    unroll = math.gcd(n_lat_tiles // n_sub, ATT_UNROLL)
    tq_lat = math.gcd(t - n_ctx, ATT_LATENT_TQ)
    out_ctx = _attention_call(bound, q_t[..., :n_ctx], k, v_t, n_ctx_tiles, 0, n_sub, unroll, ATT_TILE,
                              "attention_context")
    out_lat = _attention_call(bound, q_t[..., n_ctx:], k, v_t, n_ctx_tiles, n_lat_tiles // n_sub, n_sub, unroll,
                              tq_lat, "attention_latent")
    return out_ctx, out_lat


S5_KG = SSM_W // V7X_LANES
S5_GPK = SSM_GROUPS // S5_KG
S5_HALF = S5_GPK * SSM_STATE
S5_COLS = 2 * SSM_GROUPS * SSM_STATE
S5_PAIRS = SSM_GROUPS * SSM_STATE // V7X_LANES
S5_LEVELS = (1, 2, 4)


def _s5_disc_kernel(lr_ref, li_ref, ldt_ref, br_ref, bi_ref, pr_ref, pi_ref, bbr_ref, bbi_ref):
    lr = jnp.minimum(lr_ref[...], -1e-4)
    li = li_ref[...]
    dt = jnp.exp(ldt_ref[...])
    mag = jnp.exp(lr * dt)
    ar, ai = mag * jnp.cos(li * dt), mag * jnp.sin(li * dt)
    den = lr * lr + li * li
    gr = ((ar - 1.0) * lr + ai * li) / den
    gi = (ai * lr - (ar - 1.0) * li) / den
    br, bi = br_ref[...], bi_ref[...]
    bbr_ref[...] = gr * br - gi * bi
    bbi_ref[...] = gr * bi + gi * br
    pr, pi = ar, ai
    for k in range(V7X_SUBLANES):
        pr_ref[k] = pr
        pi_ref[k] = pi
        pr, pi = pr * ar - pi * ai, pr * ai + pi * ar


def s5_discretize(lam_re, lam_im, log_dt, b_re, b_im):
    depth = lam_re.shape[0]
    ld = depth * 2
    g, p, j = SSM_GROUPS, SSM_STATE, SSM_GROUP
    flat = lambda a: a.reshape((ld,) + a.shape[2:])
    g1p = lambda a: a.reshape(ld, g, 1, p)
    ldt = jnp.broadcast_to(flat(log_dt)[:, :, None, None], (ld, g, 1, p))
    b_t = lambda a: jnp.swapaxes(flat(a), 2, 3)
    gp = pl.BlockSpec((None, g, 1, p), lambda i: (i, 0, 0, 0))
    gjp = pl.BlockSpec((None, g, j, p), lambda i: (i, 0, 0, 0))
    pw = pl.BlockSpec((None, V7X_SUBLANES, g, 1, p), lambda i: (i, 0, 0, 0, 0))
    pow_r, pow_i, bbr, bbi = pl.pallas_call(
        _s5_disc_kernel,
        grid=(ld,),
        in_specs=[gp, gp, gp, gjp, gjp],
        out_specs=[pw, pw, gjp, gjp],
        out_shape=[jax.ShapeDtypeStruct((ld, V7X_SUBLANES, g, 1, p), _F32)] * 2
        + [jax.ShapeDtypeStruct((ld, g, j, p), _F32)] * 2,
        compiler_params=_cparams("parallel"),
        name="s5_discretize",
    )(g1p(lam_re), g1p(lam_im), ldt, b_t(b_re), b_t(b_im))
    return pow_r.reshape(ld, V7X_SUBLANES, g, p), pow_i.reshape(ld, V7X_SUBLANES, g, p), bbr, bbi


def pack_s5(pow_r, pow_i, bbr, bbi, c_re, c_im):
    ld = pow_r.shape[0]
    n = SSM_GROUPS * SSM_STATE
    rev = (jnp.arange(ld) % 2 == 1)[:, None, None]
    t = jnp.arange(V7X_SUBLANES)[None, :, None]
    pr, pi = pow_r.reshape(ld, V7X_SUBLANES, n), pow_i.reshape(ld, V7X_SUBLANES, n)
    consts = []
    for s in S5_LEVELS:
        keep = jnp.where(rev, t < V7X_SUBLANES - s, t >= s)
        for a in (pr, pi):
            consts.append(jnp.where(keep, a[:, s - 1:s, :], 0.0))
    for a in (pr, pi):
        consts.append(jnp.where(rev, a[:, ::-1, :], a))
    consts = jnp.stack(consts, 1)
    eye = jnp.eye(S5_GPK, dtype=_F32)
    bb = jnp.stack([bbr, bbi], 1).reshape(ld, 2, S5_KG, S5_GPK, SSM_GROUP, SSM_STATE)
    bd = jnp.einsum("dckgjp,gh->dkgjchp", bb, eye).reshape(ld, S5_KG, V7X_LANES, 2 * S5_HALF).astype(_BF16)
    cc = jnp.stack([c_re, -c_im], 2)
    cc = cc.reshape(ld, 2, S5_KG, S5_GPK, SSM_GROUP, SSM_STATE)
    cd = jnp.einsum("dckgjp,gh->dkcgphj", cc, eye).reshape(ld, S5_KG, 2 * S5_HALF, V7X_LANES).astype(_BF16)
    return consts, bd, cd


def _s5_scan_kernel(u_ref, bd_ref, cd_ref, k_ref, *rest, reverse, finish):
    if finish:
        yf_ref, d_ref, wg_ref, bg_ref, o_ref, bu_ref, carry_ref = rest
    else:
        o_ref, bu_ref, carry_ref = rest
    n_rows = u_ref.shape[0]

    @pl.when(pl.program_id(1) == 0)
    def _():
        carry_ref[...] = jnp.zeros_like(carry_ref)

    u = u_ref[...]
    for kg in range(S5_KG):
        bu_ref[:, kg * 2 * S5_HALF:(kg + 1) * 2 * S5_HALF] = _dot(u[:, kg * V7X_LANES:(kg + 1) * V7X_LANES], bd_ref[kg])

    n_vregs = n_rows // V7X_SUBLANES
    last = 0 if reverse else V7X_SUBLANES - 1

    def step(r, _):
        row = pl.multiple_of((n_vregs - 1 - r if reverse else r) * V7X_SUBLANES, V7X_SUBLANES)
        for jp in range(S5_PAIRS):
            kg, q = divmod(jp, S5_PAIRS // S5_KG)
            cr = kg * 2 * S5_HALF + q * V7X_LANES
            ci = cr + S5_HALF
            cl = jp * V7X_LANES
            xr = bu_ref[pl.ds(row, V7X_SUBLANES), cr:cr + V7X_LANES]
            xi = bu_ref[pl.ds(row, V7X_SUBLANES), ci:ci + V7X_LANES]
            for lvl, s in enumerate(S5_LEVELS):
                shift = V7X_SUBLANES - s if reverse else s
                sr, si = pltpu.roll(xr, shift, 0), pltpu.roll(xi, shift, 0)
                ar = k_ref[2 * lvl, :, cl:cl + V7X_LANES]
                ai = k_ref[2 * lvl + 1, :, cl:cl + V7X_LANES]
                xr, xi = xr + (ar * sr - ai * si), xi + (ar * si + ai * sr)
            pr = k_ref[2 * len(S5_LEVELS), :, cl:cl + V7X_LANES]
            pi = k_ref[2 * len(S5_LEVELS) + 1, :, cl:cl + V7X_LANES]
            hr0 = carry_ref[:, cr:cr + V7X_LANES]
            hi0 = carry_ref[:, ci:ci + V7X_LANES]
            xr, xi = xr + (pr * hr0 - pi * hi0), xi + (pr * hi0 + pi * hr0)
            bu_ref[pl.ds(row, V7X_SUBLANES), cr:cr + V7X_LANES] = xr
            bu_ref[pl.ds(row, V7X_SUBLANES), ci:ci + V7X_LANES] = xi
            carry_ref[:, cr:cr + V7X_LANES] = jnp.broadcast_to(xr[last:last + 1, :], xr.shape)
            carry_ref[:, ci:ci + V7X_LANES] = jnp.broadcast_to(xi[last:last + 1, :], xi.shape)
        return 0

    lax.fori_loop(0, n_vregs, step, 0)

    ys = [_dot(bu_ref[:, kg * 2 * S5_HALF:(kg + 1) * 2 * S5_HALF].astype(_BF16), cd_ref[kg]) for kg in range(S5_KG)]
    y = jnp.concatenate(ys, -1)
    if finish:
        y = y + yf_ref[...] + d_ref[...] * u.astype(_F32)
        z = jax.nn.gelu(y)
        o_ref[...] = (z * jax.nn.sigmoid(_dot(z.astype(_BF16), wg_ref[...]) + bg_ref[...])).astype(o_ref.dtype)
    else:
        o_ref[...] = y


def s5_scan(u, consts, bd, cd, layer, n_ctx_tiles, reverse, finish_args=None):
    bsz, t, _ = u.shape
    n_tiles = t // ROW_TILE
    d = 2 * layer + (1 if reverse else 0)
    if reverse:
        chunk = lambda i: jnp.where(i < n_ctx_tiles, n_ctx_tiles - 1 - i, n_tiles - 1 - (i - n_ctx_tiles))
    else:
        chunk = lambda i: i
    row = lambda w: pl.BlockSpec((None, ROW_TILE, w), lambda b, i: (b, chunk(i), 0))
    lay = lambda *shape: pl.BlockSpec((None,) + shape, lambda b, i: (d,) + (0,) * len(shape))
    in_specs = [row(SSM_W), lay(S5_KG, V7X_LANES, 2 * S5_HALF), lay(S5_KG, 2 * S5_HALF, V7X_LANES),
                lay(2 * len(S5_LEVELS) + 2, V7X_SUBLANES, SSM_GROUPS * SSM_STATE)]
    args = [u, bd, cd, consts]
    finish = finish_args is not None
    if finish:
        y_fwd, d_skip, w_glu, b_glu = finish_args
        lyr = lambda *shape: pl.BlockSpec((None,) + shape, lambda b, i: (layer,) + (0,) * len(shape))
        in_specs += [row(SSM_W), lyr(1, SSM_W), lyr(SSM_W, SSM_W), lyr(1, SSM_W)]
        args += [y_fwd, d_skip, w_glu, b_glu]
    return pl.pallas_call(
        functools.partial(_s5_scan_kernel, reverse=reverse, finish=finish),
        grid=(bsz, n_tiles),
        in_specs=in_specs,
        out_specs=row(SSM_W),
        out_shape=jax.ShapeDtypeStruct((bsz, t, SSM_W), _BF16 if finish else _F32),
        scratch_shapes=[
            pltpu.VMEM((ROW_TILE, S5_COLS), _F32),
            pltpu.VMEM((V7X_SUBLANES, S5_COLS), _F32),
        ],
        compiler_params=_cparams("parallel", "arbitrary"),
        name="s5_scan_bwd" if reverse else "s5_scan_fwd",
    )(*args)


HY_CT = 512
HY_FEAT_PAD = V7X_LANES
HY_TAP_COLS = HY_ORDER * HY_W


def _short_conv_kernel(prev_ref, u_ref, next_ref, w_ref, b_ref, *out_refs, n_ctx, n_total):
    i = pl.program_id(1)
    u = u_ref[...].astype(_F32)
    rows = u.shape[0]
    r = lax.broadcasted_iota(jnp.int32, u.shape, 0)
    g = r + i * rows
    before = jnp.broadcast_to(prev_ref[V7X_SUBLANES - 1:V7X_SUBLANES, :].astype(_F32), u.shape)
    after = jnp.broadcast_to(next_ref[0:1, :].astype(_F32), u.shape)
    up = jnp.where(r == 0, before, pltpu.roll(u, 1, 0))
    un = jnp.where(r == rows - 1, after, pltpu.roll(u, rows - 1, 0))
    up = jnp.where((g == 0) | (g == n_ctx), 0.0, up)
    un = jnp.where((g == n_ctx - 1) | (g == n_total - 1), 0.0, un)
    us = w_ref[0:1, :] * up + w_ref[1:2, :] * u + w_ref[2:3, :] * un + b_ref[...]
    parts = [us[:, k * HY_W:(k + 1) * HY_W] for k in range(3)]

    @pl.when(i * rows < n_ctx)
    def _():
        for ref, part in zip(out_refs[:3], parts):
            ref[...] = part.astype(ref.dtype)

    @pl.when(i * rows >= n_ctx)
    def _():
        for ref, part in zip(out_refs[3:], parts):
            ref[...] = part.astype(ref.dtype)


def hyena_short_conv(hy, conv_w, conv_b, layer, n_ctx):
    bsz, t, width = hy.shape
    per = ROW_TILE // V7X_SUBLANES
    n_halo = t // V7X_SUBLANES
    n_ctx_tiles = n_ctx // ROW_TILE
    row = lambda w: pl.BlockSpec((None, ROW_TILE, w), lambda b, i: (b, i, 0))
    return pl.pallas_call(
        functools.partial(_short_conv_kernel, n_ctx=n_ctx, n_total=t),
        grid=(bsz, t // ROW_TILE),
        in_specs=[
            pl.BlockSpec((None, V7X_SUBLANES, width), lambda b, i: (b, jnp.maximum(i * per - 1, 0), 0)),
            row(width),
            pl.BlockSpec((None, V7X_SUBLANES, width), lambda b, i: (b, jnp.minimum((i + 1) * per, n_halo - 1), 0)),
            pl.BlockSpec((None, conv_w.shape[1], width), lambda b, i: (layer, 0, 0)),
            pl.BlockSpec((None, 1, width), lambda b, i: (layer, 0, 0)),
        ],
        out_specs=[pl.BlockSpec((None, ROW_TILE, HY_W), lambda b, i: (b, jnp.minimum(i, n_ctx_tiles - 1), 0))] * 3
        + [pl.BlockSpec((None, ROW_TILE, HY_W), lambda b, i: (b, jnp.maximum(i - n_ctx_tiles, 0), 0))] * 3,
        out_shape=[jax.ShapeDtypeStruct((bsz, n_ctx, HY_W), _BF16)] * 3
        + [jax.ShapeDtypeStruct((bsz, t - n_ctx, HY_W), _BF16)] * 3,
        compiler_params=_cparams("arbitrary", "arbitrary"),
        name="hyena_short_conv",
    )(hy, hy, hy, conv_w, conv_b)


def hyena_features(n):
    m = jnp.arange(2 * n, dtype=jnp.int32)
    t = jnp.where(m < n, m, jnp.where(m == n, 0, 2 * n - m)).astype(_F32)
    t_norm = t / (n - 1)
    bands = jnp.linspace(1e-4, HY_BANDS - 1, HY_BANDS, dtype=_F32)
    ang = (2.0 * math.pi * t / n)[:, None] * bands[None, :]
    feat = jnp.concatenate([t_norm[:, None], jnp.cos(ang), -jnp.sin(ang)], -1)
    feat = jnp.pad(feat, ((0, 0), (0, HY_FEAT_PAD - HY_EMB)))
    return feat, jnp.broadcast_to(t_norm[:, None], (2 * n, V7X_LANES))


def _filter_kernel(feat_ref, tn_ref, w1_ref, b1_ref, w2_ref, b2_ref, w3_ref, fr_ref, dec_ref, h_ref, l1_ref, *, n):
    i = pl.program_id(0)
    hp = lax.Precision.HIGHEST
    fr = fr_ref[...]
    h = jnp.sin(fr * (jnp.dot(feat_ref[...], w1_ref[...], preferred_element_type=_F32, precision=hp) + b1_ref[...]))
    h = jnp.sin(fr * (jnp.dot(h, w2_ref[...], preferred_element_type=_F32, precision=hp) + b2_ref[...]))
    h = _dot(h.astype(_BF16), w3_ref[...])
    tn = tn_ref[...]
    rows = h.shape[0]
    zero_tap = (lax.broadcasted_iota(jnp.int32, (rows, V7X_LANES), 0) + i * rows) == n

    @pl.when(i == 0)
    def _():
        l1_ref[...] = jnp.zeros_like(l1_ref)

    for c in range(HY_TAP_COLS // V7X_LANES):
        sl = slice(c * V7X_LANES, (c + 1) * V7X_LANES)
        blk = jnp.where(zero_tap, 0.0, h[:, sl] * jnp.exp(-tn * jnp.abs(dec_ref[:, sl])))
        h_ref[:, sl] = blk.astype(h_ref.dtype)
        l1_ref[:, sl] += jnp.sum(jnp.abs(blk).reshape(rows // V7X_SUBLANES, V7X_SUBLANES, V7X_LANES), 0)


def hyena_filter_taps(n, w1, b1, w2, b2, w3_dir, freq, decay_dir, layer):
    feat, tn = hyena_features(n)
    rows = min(ROW_TILE, n)
    half = n // rows
    lay = lambda *shape: pl.BlockSpec((None,) + shape, lambda i: (layer,) + (0,) * len(shape))
    by_dir = lambda *shape: pl.BlockSpec((None, None) + shape,
                                         lambda i: (layer, jnp.where(i >= half, 1, 0)) + (0,) * len(shape))
    return pl.pallas_call(
        functools.partial(_filter_kernel, n=n),
        grid=(2 * half,),
        in_specs=[
            pl.BlockSpec((rows, HY_FEAT_PAD), lambda i: (i, 0)),
            pl.BlockSpec((rows, V7X_LANES), lambda i: (i, 0)),
            lay(HY_FEAT_PAD, HY_HIDDEN), lay(1, HY_HIDDEN), lay(HY_HIDDEN, HY_HIDDEN), lay(1, HY_HIDDEN),
            by_dir(HY_HIDDEN, HY_TAP_COLS), lay(1, HY_HIDDEN), by_dir(1, HY_TAP_COLS),
        ],
        out_specs=[pl.BlockSpec((rows, HY_TAP_COLS), lambda i: (i, 0)),
                   pl.BlockSpec((V7X_SUBLANES, HY_TAP_COLS), lambda i: (0, 0))],
        out_shape=[jax.ShapeDtypeStruct((2 * n, HY_TAP_COLS), _BF16),
                   jax.ShapeDtypeStruct((V7X_SUBLANES, HY_TAP_COLS), _F32)],
        compiler_params=_cparams("arbitrary"),
        name="hyena_filter_taps",
    )(feat, tn, w1, b1, w2, b2, w3_dir, freq, decay_dir)


def _dft_block(n_out, n_in, sign, scale=1.0, real_input=False):
    size = max(n_out, n_in)
    ang = 2.0 * np.pi * np.outer(np.arange(n_out), np.arange(n_in)) / size
    fr, fi = np.cos(ang) * scale, sign * np.sin(ang) * scale
    blk = np.concatenate([fr, fi], 0) if real_input else np.block([[fr, -fi], [fi, fr]])
    return jnp.asarray(blk, _F32).astype(_BF16)


def _twiddle_table(n1, n2, sign):
    ang = 2.0 * np.pi * np.outer(np.arange(n1), np.arange(n2)) / (n1 * n2)
    tw = np.stack([np.cos(ang), sign * np.sin(ang)], 0)[..., None]
    return jnp.broadcast_to(jnp.asarray(tw, _F32), (2, n1, n2, V7X_LANES))


def _cmul(ar, ai, br, bi):
    return ar * br - ai * bi, ar * bi + ai * br


def _lanes_mul(xr, xi, twr, twi):
    outs_r, outs_i = [], []
    for c in range(xr.shape[1] // V7X_LANES):
        sl = slice(c * V7X_LANES, (c + 1) * V7X_LANES)
        r, i = _cmul(xr[:, sl], xi[:, sl], twr, twi)
        outs_r.append(r)
        outs_i.append(i)
    return jnp.concatenate(outs_r, 1), jnp.concatenate(outs_i, 1)


def _dft_kernel(*refs, real_input, pre_tw, has_filter, has_second, post_tw, slabs):
    refs = list(refs)
    xr_ref = refs.pop(0)
    xi_ref = None if real_input else refs.pop(0)
    f_ref = refs.pop(0)
    tw_ref = refs.pop(0) if (pre_tw or post_tw) else None
    if has_filter:
        hr_ref, hi_ref, l1_ref = refs.pop(0), refs.pop(0), refs.pop(0)
    g_ref = refs.pop(0) if has_second else None
    or_ref, oi_ref = refs
    for a in range(slabs):
        xr = xr_ref[a].astype(_F32)
        if real_input:
            s = xr.astype(_BF16)
        else:
            xi = xi_ref[a].astype(_F32)
            if pre_tw:
                xr, xi = _lanes_mul(xr, xi, tw_ref[0, a], tw_ref[1, a])
            s = jnp.concatenate([xr, xi], 0).astype(_BF16)
        y = _dot(f_ref[...], s)
        m = y.shape[0] // 2
        yr, yi = y[:m], y[m:]
        if has_filter:
            inv = 1.0 / (jnp.sum(l1_ref[...], 0, keepdims=True) + EPS)
            yr, yi = _cmul(yr, yi, hr_ref[a].astype(_F32) * inv, hi_ref[a].astype(_F32) * inv)
        if has_second:
            z = _dot(g_ref[...], jnp.concatenate([yr, yi], 0).astype(_BF16))
            m = z.shape[0] // 2
            yr, yi = z[:m], z[m:]
        if post_tw:
            yr, yi = _lanes_mul(yr, yi, tw_ref[0, a], -tw_ref[1, a])
        or_ref[a] = yr.astype(or_ref.dtype)
        oi_ref[a] = yi.astype(oi_ref.dtype)


def dft_apply(xr, xi, f_blk, *, tw=None, pre_tw=False, post_tw=False, filt=None, g_blk=None, out_dtype=_BF16):
    n_a, k, c = xr.shape
    real_input = xi is None
    m1 = f_blk.shape[0] // 2
    m_out = g_blk.shape[0] // 2 if g_blk is not None else m1
    ct = min(HY_CT, c)
    slabs = max(1, min(n_a, 1024 // max(k, m_out)))
    slab_spec = lambda rows: pl.BlockSpec((slabs, rows, ct), lambda j, a: (a, 0, j))
    whole = lambda arr: pl.BlockSpec(arr.shape, lambda j, a: (0,) * arr.ndim)
    in_specs, args = [slab_spec(k)], [xr]
    if not real_input:
        in_specs.append(slab_spec(k))
        args.append(xi)
    in_specs.append(whole(f_blk))
    args.append(f_blk)
    if pre_tw or post_tw:
        in_specs.append(pl.BlockSpec((2, slabs, tw.shape[2], V7X_LANES), lambda j, a: (0, a, 0, 0)))
        args.append(tw)
    if filt is not None:
        h_re, h_im, l1, first_block = filt
        h_spec = pl.BlockSpec((slabs, m1, ct), lambda j, a: (a, 0, j + first_block))
        l1_spec = pl.BlockSpec((V7X_SUBLANES, ct), lambda j, a: (0, j + first_block))
        in_specs += [h_spec, h_spec, l1_spec]
        args += [h_re, h_im, l1]
    if g_blk is not None:
        in_specs.append(whole(g_blk))
        args.append(g_blk)
    kern = functools.partial(_dft_kernel, real_input=real_input, pre_tw=pre_tw, has_filter=filt is not None,
                             has_second=g_blk is not None, post_tw=post_tw, slabs=slabs)
    return pl.pallas_call(
        kern,
        grid=(c // ct, n_a // slabs),
        in_specs=in_specs,
        out_specs=[slab_spec(m_out)] * 2,
        out_shape=[jax.ShapeDtypeStruct((n_a, m_out, c), out_dtype)] * 2,
        compiler_params=_cparams("parallel", "parallel"),
        name="hyena_dft",
    )(*args)


def _fft_factors(n_fft):
    n1 = 1 << ((n_fft.bit_length() - 1) // 2)
    return n_fft // n1, n1


def filter_spectrum(taps):
    n_fft, c = taps.shape
    if n_fft <= 512:
        return dft_apply(taps[None], None, _dft_block(n_fft, n_fft, -1.0, real_input=True))
    n1, n2 = _fft_factors(n_fft)
    x = jnp.swapaxes(taps.reshape(n1, n2, c), 0, 1)
    yr, yi = dft_apply(x, None, _dft_block(n1, n1, -1.0, real_input=True))
    yr, yi = jnp.swapaxes(yr, 0, 1), jnp.swapaxes(yi, 0, 1)
    return dft_apply(yr, yi, _dft_block(n2, n2, -1.0), tw=_twiddle_table(n1, n2, -1.0), pre_tw=True)


def long_conv_pair(vr, vi, filt):
    n, c = vr.shape
    n_fft = 2 * n
    if n_fft <= 512:
        f = _dft_block(n_fft, n, -1.0)
        g = _dft_block(n, n_fft, 1.0, scale=1.0 / n_fft)
        yr, yi = dft_apply(vr[None], vi[None], f, filt=filt, g_blk=g)
        return yr[0], yi[0]
    n1, n2 = _fft_factors(n_fft)
    n1h = n1 // 2
    to_slabs = lambda a: jnp.swapaxes(a.reshape(n1h, n2, c), 0, 1)
    yr, yi = dft_apply(to_slabs(vr), to_slabs(vi), _dft_block(n1, n1h, -1.0))
    yr, yi = jnp.swapaxes(yr, 0, 1), jnp.swapaxes(yi, 0, 1)
    tw = _twiddle_table(n1, n2, -1.0)
    zr, zi = dft_apply(yr, yi, _dft_block(n2, n2, -1.0), tw=tw, pre_tw=True, post_tw=True, filt=filt,
                       g_blk=_dft_block(n2, n2, 1.0))
    zr, zi = jnp.swapaxes(zr, 0, 1), jnp.swapaxes(zi, 0, 1)
    outr, outi = dft_apply(zr, zi, _dft_block(n1h, n1, 1.0, scale=1.0 / n_fft))
    back = lambda a: jnp.swapaxes(a, 0, 1).reshape(n, c)
    return back(outr), back(outi)


def _gate_kernel(x_ref, conv_ref, v_ref, skip_ref, o_ref):
    v = v_ref[...].astype(_F32)
    o_ref[...] = (x_ref[...].astype(_F32) * (conv_ref[...].astype(_F32) + skip_ref[...] * v)).astype(o_ref.dtype)


def hyena_gate(xg, conv, v, skip, layer, order):
    bsz, n, _ = xg.shape
    rows = min(ROW_TILE, n)
    row = pl.BlockSpec((None, rows, HY_W), lambda b, i: (b, i, 0))
    return pl.pallas_call(
        _gate_kernel,
        grid=(bsz, n // rows),
        in_specs=[row, row, row, pl.BlockSpec((None, None, 1, HY_W), lambda b, i: (layer, order, 0, 0))],
        out_specs=row,
        out_shape=jax.ShapeDtypeStruct((bsz, n, HY_W), _BF16),
        compiler_params=_cparams("parallel", "parallel"),
        name="hyena_gate",
    )(xg, conv, v, skip)


def hyena_segment(v, x1, x2, w1, b1, w2, b2, w3, freq, decay, skip, layer):
    bsz, n, _ = v.shape
    assert bsz == 2, "the two batch elements are packed into one complex transform"
    taps, l1 = hyena_filter_taps(n, w1, b1, w2, b2, w3, freq, decay, layer)
    hr, hi = filter_spectrum(taps)
    u = v
    for o, xg in enumerate((x1, x2)):
        filt = (hr, hi, l1, o * (HY_W // min(HY_CT, HY_W)))
        c0, c1 = long_conv_pair(u[0], u[1], filt)
        u = hyena_gate(xg, jnp.stack([c0, c1], 0), u, skip, layer, o)
    return u


def pack_hyena_weights(p):
    depth = p["hy_w1"].shape[0]
    w1 = jnp.pad(p["hy_w1"], ((0, 0), (0, HY_FEAT_PAD - HY_EMB), (0, 0)))
    row = lambda a: a.reshape(depth, 1, -1)
    w3 = p["hy_w3"].reshape(depth, HY_HIDDEN, HY_ORDER, 2, HY_W)
    w3_dir = jnp.transpose(w3, (0, 3, 1, 2, 4)).reshape(depth, 2, HY_HIDDEN, HY_TAP_COLS).astype(_BF16)
    decay_dir = jnp.swapaxes(p["hy_decay"], 1, 2).reshape(depth, 2, 1, HY_TAP_COLS)
    return (w1, row(p["hy_b1"]), p["hy_w2"], row(p["hy_b2"]), w3_dir, row(p["hy_freq"]), decay_dir,
            p["hy_skip"].reshape(depth, HY_ORDER, 1, HY_W))


def kernel(x, c, ctx, c_ctx, w_mod, b_mod, norm1, norm2, w_in, ssm_lam_re, ssm_lam_im, ssm_log_dt, ssm_b_re, ssm_b_im, ssm_c_re, ssm_c_im, ssm_d, ssm_w_glu, ssm_b_glu, mla_q_norm, mla_w_uq, mla_kv_norm, mla_w_ukv, qk_norm_q, qk_norm_k, hy_conv_w, hy_conv_b, hy_w1, hy_b1, hy_w2, hy_b2, hy_w3, hy_freq, hy_decay, hy_skip, w_branch, w_out, router_w, router_bias, moe_w_gate, moe_w_up, moe_w_down):
    depth = w_mod.shape[0]
    bsz, n_lat, _ = x.shape
    n_ctx = ctx.shape[1]
    assert n_ctx % ROW_TILE == 0 and n_lat % ROW_TILE == 0 and ROW_TILE == ATT_TILE
    n_ctx_tiles = n_ctx // ROW_TILE
    row = lambda a: a.reshape(depth, 1, -1)

    mod_tab = modulation_table(c, c_ctx, w_mod, b_mod)
    w_in_used = pack_w_in(w_in)
    w_gates = w_in[:, :, _IN_GATES:].astype(_BF16)
    w_branch_b, w_out_b = w_branch.astype(_BF16), w_out.astype(_BF16)
    moe_b = (moe_w_gate.astype(_BF16), moe_w_up.astype(_BF16), moe_w_down.astype(_BF16))
    rw_hi, rw_lo, rb_col = split_router(router_w, router_bias)
    norm1_r, norm2_r = row(norm1), row(norm2)
    s5_consts, s5_bd, s5_cd = pack_s5(*s5_discretize(ssm_lam_re, ssm_lam_im, ssm_log_dt, ssm_b_re, ssm_b_im),
                                      ssm_c_re, ssm_c_im)
    s5_finish = (row(ssm_d), ssm_w_glu.astype(_BF16), row(ssm_b_glu))
    mla_packed = pack_mla_weights(mla_w_uq, mla_w_ukv, qk_norm_q, qk_norm_k)
    mla_tables = rope_tables(n_ctx, n_lat)
    att_bounds = score_bound(qk_norm_q, qk_norm_k)
    hy_w = pack_hyena_weights(dict(hy_w1=hy_w1, hy_b1=hy_b1, hy_w2=hy_w2, hy_b2=hy_b2, hy_w3=hy_w3, hy_freq=hy_freq,
                                   hy_decay=hy_decay, hy_skip=hy_skip))
    hy_conv_b_r = row(hy_conv_b)

    xcat = jnp.concatenate([ctx, x], 1)
    for l in range(depth):
        u, q_lat, kv_lat, hy, k_rope = input_projection(xcat, mod_tab, norm1_r, w_in_used, l, n_ctx_tiles)

        y_fwd = s5_scan(u, s5_consts, s5_bd, s5_cd, l, n_ctx_tiles, False)
        y_ssm = s5_scan(u, s5_consts, s5_bd, s5_cd, l, n_ctx_tiles, True, (y_fwd,) + s5_finish)

        q_t, k, v_t = mla_prepare(q_lat, kv_lat, k_rope, row(mla_q_norm), row(mla_kv_norm), mla_packed, mla_tables, l)
        att_ctx, att_lat = (jnp.swapaxes(a, 1, 2)
                            for a in flash_attention(q_t, k, v_t, n_ctx_tiles, att_bounds[l:l + 1]))

        v_c, x1_c, x2_c, v_l, x1_l, x2_l = hyena_short_conv(hy, hy_conv_w, hy_conv_b_r, l, n_ctx)
        hy_lat = hyena_segment(v_l, x1_l, x2_l, *hy_w, l)
        if l < depth - 1:
            hy_ctx = hyena_segment(v_c, x1_c, x2_c, *hy_w, l)
        else:
            hy_ctx = jnp.zeros((bsz, n_ctx, HY_W), hy_lat.dtype)

        x_new, hl, route, counts = merge_and_route(xcat, y_ssm, att_ctx, att_lat, hy_ctx, hy_lat, mod_tab, norm1_r,
                                                   norm2_r, w_gates, w_branch_b, w_out_b, rw_hi, rw_lo, rb_col, l,
                                                   n_ctx_tiles)
        xcat = moe_block(x_new, hl, route, counts, mod_tab, *moe_b, l, n_ctx_tiles)
    return xcat[:, n_ctx:]
```
